```python
import math
import jax, jax.numpy as jnp
from jax import lax
import numpy as np

D_MODEL = 1024
BATCH = 32
SEQ = 256
DEPTH = 2
DEC_BATCH = 4
DEC_SEQ = 4096
PAST_LEN = 256

GRID_W = 64
N_EVEN = (DEPTH + 1) // 2
N_ODD = DEPTH // 2
MLA_HEADS = 8
Q_LORA = 384
KV_LORA = 256
QK_NOPE = 64
QK_ROPE = 32
QK_HEAD = QK_NOPE + QK_ROPE
V_HEAD = 64
ROPE_THETA = 10000.0
Q_BLOCK = 128
GLA_HEADS = 4
GLA_DK = 64
GLA_DV = 128
ALPHA_RANK = 16
GLA_TAU = 16.0
GLA_CHUNK = 64
CONV_WIDTH = 31
CONV_PAD = CONV_WIDTH // 2
FFN_HIDDEN = ((8 * D_MODEL // 3 + 255) // 256) * 256
MIX_OUT = MLA_HEADS * V_HEAD + GLA_HEADS * GLA_DV
IN_SPLITS = (Q_LORA, KV_LORA, QK_ROPE, GLA_HEADS * GLA_DK, GLA_HEADS * GLA_DK,
             GLA_HEADS * GLA_DV, GLA_HEADS * GLA_DV, 2 * ALPHA_RANK)
IN_WIDTH = sum(IN_SPLITS)
EPS = 1e-6

kernel_name = 'hybrid_mla_gla_conformer_dit_step'


def rmsnorm(x, g):
    xf = x.astype(jnp.float32)
    y = xf * lax.rsqrt(jnp.mean(xf * xf, axis=-1, keepdims=True) + EPS)
    return (y * g.astype(jnp.float32)).astype(x.dtype)


def layernorm(x, g, b):
    xf = x.astype(jnp.float32)
    mu = jnp.mean(xf, axis=-1, keepdims=True)
    var = jnp.mean(jnp.square(xf - mu), axis=-1, keepdims=True)
    y = (xf - mu) * lax.rsqrt(var + EPS)
    return (y * g.astype(jnp.float32) + b.astype(jnp.float32)).astype(x.dtype)


def ada_params(cvec, w, b):
    m = jax.nn.silu(cvec) @ w + b
    m = m.reshape(m.shape[:-1] + (6, D_MODEL))
    if m.ndim == 3:
        m = m[:, None]
    return [m[..., i, :] for i in range(6)]


def modulate(x, g, shift, scale):
    return rmsnorm(x, g) * (1 + scale) + shift


def axial_rope_angles(n_tokens):
    rows = n_tokens // GRID_W
    r = jnp.repeat(jnp.arange(rows), GRID_W).astype(jnp.float32)
    col = jnp.tile(jnp.arange(GRID_W), rows).astype(jnp.float32)
    half = QK_ROPE // 2
    inv_freq = ROPE_THETA ** (-jnp.arange(0, half, 2, dtype=jnp.float32) / half)
    return r[:, None] * inv_freq[None, :], col[:, None] * inv_freq[None, :]


def rotate(x, ang):
    m = ang.shape[-1]
    cos = jnp.cos(ang)[:, None, :].astype(x.dtype)
    sin = jnp.sin(ang)[:, None, :].astype(x.dtype)
    x1, x2 = x[..., :m], x[..., m:]
    return jnp.concatenate([x1 * cos - x2 * sin, x2 * cos + x1 * sin], axis=-1)


def apply_axial_rope(x, angles):
    ang_r, ang_c = angles
    h = QK_ROPE // 2
    return jnp.concatenate([rotate(x[..., :h], ang_r), rotate(x[..., h:], ang_c)], axis=-1)


def blocked_attend(q, k, v):
    b, lq, h, d = q.shape
    nb = lq // Q_BLOCK
    scale = 1.0 / math.sqrt(d)
    qb = q.reshape(b, nb, Q_BLOCK, h, d).transpose(1, 0, 2, 3, 4)

    def one_block(qi):
        s = jnp.einsum('bqhd,bkhd->bhqk', qi, k).astype(jnp.float32) * scale
        p = jax.nn.softmax(s, axis=-1).astype(v.dtype)
        return jnp.einsum('bhqk,bkhd->bqhd', p, v)

    o = lax.map(one_block, qb)
    return o.transpose(1, 0, 2, 3, 4).reshape(b, lq, h * v.shape[-1])


def gla_scan(q, k, v, log_a, s0):
    b, l, h, dk = q.shape
    dv = v.shape[-1]
    nc = l // GLA_CHUNK

    def chunks(t):
        return t.astype(jnp.float32).reshape(b, nc, GLA_CHUNK, h, t.shape[-1]).transpose(1, 0, 3, 2, 4)

    causal = jnp.tril(jnp.ones((GLA_CHUNK, GLA_CHUNK), dtype=bool))[:, :, None]

    def step(state, inp):
        qc, kc, vc, gc = inp
        cum = jnp.cumsum(gc, axis=2)
        decay = jnp.exp(jnp.where(causal, cum[:, :, :, None, :] - cum[:, :, None, :, :], -jnp.inf))
        scores = jnp.einsum('bhid,bhjd,bhijd->bhij', qc, kc, decay)
        o = jnp.einsum('bhij,bhjv->bhiv', scores, vc) + jnp.einsum('bhid,bhdv->bhiv', qc * jnp.exp(cum), state)
        last = cum[:, :, -1, :]
        state = jnp.exp(last)[..., None] * state + jnp.einsum(
            'bhjd,bhjv->bhdv', kc * jnp.exp(last[:, :, None, :] - cum), vc)
        return state, o

    s_fin, o = lax.scan(step, s0.astype(jnp.float32), (chunks(q), chunks(k), chunks(v), chunks(log_a)))
    o = o.transpose(1, 0, 3, 2, 4).reshape(b, l, h, dv)
    return o.astype(v.dtype), s_fin


def gla_bidir(q, k, v, la_f, la_b, s0_f, s0_b):
    flip = lambda t: jnp.flip(t, axis=1)
    o_f, s_f = gla_scan(q, k, v, la_f, s0_f)
    o_b, s_b = gla_scan(flip(q), flip(k), flip(v), flip(la_b), s0_b)
    return o_f + flip(o_b), s_f, s_b


def split_in(proj):
    outs, start = [], 0
    for w in IN_SPLITS:
        outs.append(proj[..., start:start + w])
        start += w
    return outs


def mla_queries(cq, qa_norm, w_uq, q_gain, angles):
    b, l, _ = cq.shape
    q = (rmsnorm(cq, qa_norm) @ w_uq).reshape(b, l, MLA_HEADS, QK_HEAD)
    q = rmsnorm(q, q_gain)
    if angles is not None:
        q = jnp.concatenate([q[..., :QK_NOPE], apply_axial_rope(q[..., QK_NOPE:], angles)], axis=-1)
    return q


def mla_keys_values(ckv_n, krope, w_ukv, k_gain, angles):
    b, l, _ = ckv_n.shape
    kv = (ckv_n @ w_ukv).reshape(b, l, MLA_HEADS, QK_NOPE + V_HEAD)
    k_pe = jnp.broadcast_to(krope[:, :, None, :], (b, l, MLA_HEADS, QK_ROPE))
    k = rmsnorm(jnp.concatenate([kv[..., :QK_NOPE], k_pe], axis=-1), k_gain)
    if angles is not None:
        k = jnp.concatenate([k[..., :QK_NOPE], apply_axial_rope(k[..., QK_NOPE:], angles)], axis=-1)
    return k, kv[..., QK_NOPE:]


def even_mixer(h, p, j, angles, ctx):
    b, l, _ = h.shape
    cq, ckv, krope, gq, gk, gv, gg, ga = split_in(h @ p['w_in'][j])
    ckv_n = rmsnorm(ckv, p['kva_norm'][j])
    q = mla_queries(cq, p['qa_norm'][j], p['w_uq'][j], p['q_norm'][j], angles)
    k, v = mla_keys_values(ckv_n, krope, p['w_ukv'][j], p['k_norm'][j], angles)
    g_q = gq.reshape(b, l, GLA_HEADS, GLA_DK) * (GLA_DK ** -0.5)
    g_k = gk.reshape(b, l, GLA_HEADS, GLA_DK)
    g_v = gv.reshape(b, l, GLA_HEADS, GLA_DV)
    la = [jax.nn.log_sigmoid((ga[..., d * ALPHA_RANK:(d + 1) * ALPHA_RANK] @ p['w_alpha_up'][j, d]
                               + p['b_alpha'][j, d]).astype(jnp.float32)).reshape(b, l, GLA_HEADS, GLA_DK) / GLA_TAU
          for d in range(2)]
    if ctx is None:
        s0_f = jnp.zeros((b, GLA_HEADS, GLA_DK, GLA_DV), jnp.float32)
        s0_b = s0_f
    else:
        ckv_c, krope_c, s_c = ctx
        k_c, v_c = mla_keys_values(ckv_c, krope_c, p['w_ukv'][j], p['k_norm'][j], None)
        k = jnp.concatenate([k_c, k], axis=1)
        v = jnp.concatenate([v_c, v], axis=1)
        s0_f, s0_b = s_c[:, 0], s_c[:, 1]
    attn = blocked_attend(q, k, v)
    o, s_f, s_b = gla_bidir(g_q, g_k, g_v, la[0], la[1], s0_f, s0_b)
    gla_out = (rmsnorm(o, p['gla_norm'][j]) * jax.nn.silu(gg.reshape(b, l, GLA_HEADS, GLA_DV))).reshape(b, l, -1)
    out = jnp.concatenate([attn, gla_out], axis=-1) @ p['w_out'][j]
    if ctx is None:
        return out, (ckv_n, krope, jnp.stack([s_f, s_b], axis=1))
    return out, None


def conv_module(h, p, j):
    u = h @ p['conv_w_pw1'][j] + p['conv_b_pw1'][j]
    u = u[..., :D_MODEL] * jax.nn.sigmoid(u[..., D_MODEL:])
    u = lax.conv_general_dilated(u, p['conv_w_dw'][j][:, None, :], window_strides=(1,),
                                 padding=[(CONV_PAD, CONV_PAD)], dimension_numbers=('NWC', 'WIO', 'NWC'),
                                 feature_group_count=D_MODEL) + p['conv_b_dw'][j]
    u = jax.nn.silu(layernorm(u, p['conv_ln_g'][j], p['conv_ln_b'][j]))
    return u @ p['conv_w_pw2'][j] + p['conv_b_pw2'][j]


def swiglu(h, p, i):
    return (jax.nn.silu(h @ p['ffn_w_gate'][i]) * (h @ p['ffn_w_up'][i])) @ p['ffn_w_down'][i]


def run_trunk(x, cvec, p, angles, caches):
    ckv_l, kr_l, st_l = [], [], []
    for i in range(DEPTH):
        j = i // 2
        sh1, sc1, g1, sh2, sc2, g2 = ada_params(cvec, p['w_ada'][i], p['b_ada'][i])
        h = modulate(x, p['norm_mix'][i], sh1, sc1)
        if i % 2 == 0:
            ctx = None if caches is None else (caches[0][:, j], caches[1][:, j], caches[2][:, j])
            out, st = even_mixer(h, p, j, angles, ctx)
            if st is not None:
                ckv_l.append(st[0])
                kr_l.append(st[1])
                st_l.append(st[2])
        else:
            out = conv_module(h, p, j)
        x = x + g1 * out
        h = modulate(x, p['norm_ffn'][i], sh2, sc2)
        x = x + g2 * swiglu(h, p, i)
    return x, ckv_l, kr_l, st_l


def setup_inputs(seed: int = 0) -> dict:
    key = jax.random.key(seed)
    ks = list(jax.random.split(key, 40))

    def nrm(shape, scale=1.0):
        return jax.random.normal(ks.pop(), shape, jnp.float32) * scale

    def gain(shape):
        return 1.0 + nrm(shape, 0.1)

    D = D_MODEL
    return {
        'x_prompt': nrm((BATCH, SEQ, D)),
        'x_sample': nrm((DEC_BATCH, DEC_SEQ, D)),
        'cache_mla_ckv': nrm((DEC_BATCH, N_EVEN, PAST_LEN, KV_LORA)),
        'cache_mla_krope': nrm((DEC_BATCH, N_EVEN, PAST_LEN, QK_ROPE)),
        'state_gla': nrm((DEC_BATCH, N_EVEN, 2, GLA_HEADS, GLA_DK, GLA_DV)),
        'c': nrm((DEC_BATCH, D)),
        'c_ctx': nrm((D,)),
        'w_ada': nrm((DEPTH, D, 6 * D), 0.5 * D ** -0.5),
        'b_ada': nrm((DEPTH, 6 * D), 0.01),
        'norm_mix': gain((DEPTH, D)),
        'norm_ffn': gain((DEPTH, D)),
        'w_in': nrm((N_EVEN, D, IN_WIDTH), D ** -0.5),
        'qa_norm': gain((N_EVEN, Q_LORA)),
        'w_uq': nrm((N_EVEN, Q_LORA, MLA_HEADS * QK_HEAD), Q_LORA ** -0.5),
        'kva_norm': gain((N_EVEN, KV_LORA)),
        'w_ukv': nrm((N_EVEN, KV_LORA, MLA_HEADS * (QK_NOPE + V_HEAD)), KV_LORA ** -0.5),
        'q_norm': gain((N_EVEN, QK_HEAD)),
        'k_norm': gain((N_EVEN, QK_HEAD)),
        'w_alpha_up': nrm((N_EVEN, 2, ALPHA_RANK, GLA_HEADS * GLA_DK), ALPHA_RANK ** -0.5),
        'b_alpha': nrm((N_EVEN, 2, GLA_HEADS * GLA_DK), 0.1),
        'gla_norm': gain((N_EVEN, GLA_DV)),
        'w_out': nrm((N_EVEN, MIX_OUT, D), MIX_OUT ** -0.5),
        'conv_w_pw1': nrm((N_ODD, D, 2 * D), D ** -0.5),
        'conv_b_pw1': nrm((N_ODD, 2 * D), 0.01),
        'conv_w_dw': nrm((N_ODD, CONV_WIDTH, D), CONV_WIDTH ** -0.5),
        'conv_b_dw': nrm((N_ODD, D), 0.01),
        'conv_ln_g': gain((N_ODD, D)),
        'conv_ln_b': nrm((N_ODD, D), 0.01),
        'conv_w_pw2': nrm((N_ODD, D, D), D ** -0.5),
        'conv_b_pw2': nrm((N_ODD, D), 0.01),
        'ffn_w_gate': nrm((DEPTH, D, FFN_HIDDEN), D ** -0.5),
        'ffn_w_up': nrm((DEPTH, D, FFN_HIDDEN), D ** -0.5),
        'ffn_w_down': nrm((DEPTH, FFN_HIDDEN, D), FFN_HIDDEN ** -0.5),
    }


def reference(x_prompt, x_sample, cache_mla_ckv, cache_mla_krope, state_gla, c, c_ctx,
              w_ada, b_ada, norm_mix, norm_ffn, w_in, qa_norm, w_uq, kva_norm, w_ukv, q_norm, k_norm,
              w_alpha_up, b_alpha, gla_norm, w_out, conv_w_pw1, conv_b_pw1, conv_w_dw, conv_b_dw,
              conv_ln_g, conv_ln_b, conv_w_pw2, conv_b_pw2, ffn_w_gate, ffn_w_up, ffn_w_down):
    p = dict(w_ada=w_ada, b_ada=b_ada, norm_mix=norm_mix, norm_ffn=norm_ffn, w_in=w_in, qa_norm=qa_norm,
             w_uq=w_uq, kva_norm=kva_norm, w_ukv=w_ukv, q_norm=q_norm, k_norm=k_norm, w_alpha_up=w_alpha_up,
             b_alpha=b_alpha, gla_norm=gla_norm, w_out=w_out, conv_w_pw1=conv_w_pw1, conv_b_pw1=conv_b_pw1,
             conv_w_dw=conv_w_dw, conv_b_dw=conv_b_dw, conv_ln_g=conv_ln_g, conv_ln_b=conv_ln_b,
             conv_w_pw2=conv_w_pw2, conv_b_pw2=conv_b_pw2, ffn_w_gate=ffn_w_gate, ffn_w_up=ffn_w_up,
             ffn_w_down=ffn_w_down)
    y_prompt, ckv_l, kr_l, st_l = run_trunk(x_prompt, c_ctx, p, None, None)
    new_mla_ckv = jnp.stack(ckv_l, axis=1)
    new_mla_krope = jnp.stack(kr_l, axis=1)
    new_gla_state = jnp.stack(st_l, axis=1).astype(x_prompt.dtype)
    angles = axial_rope_angles(x_sample.shape[1])
    y_sample, _, _, _ = run_trunk(x_sample, c, p, angles, (cache_mla_ckv, cache_mla_krope, state_gla))
    return (y_prompt, y_sample, new_mla_ckv, new_mla_krope, new_gla_state)
```

```python
import functools
import math

import numpy as np
import jax
import jax.numpy as jnp
from jax import lax
from jax.experimental import pallas as pl
from jax.experimental.pallas import tpu as pltpu

D = 1024
GRID_W = 64
MLA_HEADS = 8
Q_LORA = 384
KV_LORA = 256
QK_NOPE = 64
QK_ROPE = 32
QK_HEAD = QK_NOPE + QK_ROPE
V_HEAD = 64
ROPE_THETA = 10000.0
GLA_HEADS = 4
GLA_DK = 64
GLA_DV = 128
ALPHA_RANK = 16
GLA_TAU = 16.0
CONV_WIDTH = 31
CONV_PAD = CONV_WIDTH // 2
FFN_HIDDEN = 2816
EPS = 1e-6

LANES = 128
MXU_COLS = 256
VMEM_LIMIT = 56 * 1024 * 1024

SLAB = LANES
QK_W = MLA_HEADS * SLAB
V_W = MLA_HEADS * V_HEAD
GQ_W = GLA_HEADS * GLA_DK
GV_W = GLA_HEADS * GLA_DV
FFN_CHUNK = MXU_COLS
FFN_NCHUNK = FFN_HIDDEN // FFN_CHUNK
HALO = 16
GLA_BLK = 16
GLA_MC = 128
GLA_NBLK = GLA_MC // GLA_BLK

_OFF_CQ = 0
_OFF_CKV = _OFF_CQ + Q_LORA
_OFF_XS = _OFF_CKV + KV_LORA
_OFF_KRS = _OFF_XS + SLAB
_OFF_GQ = _OFF_KRS + SLAB
_OFF_GK = _OFF_GQ + GQ_W
_OFF_GV = _OFF_GK + GQ_W
_OFF_GG = _OFF_GV + GV_W
IN_W_PAD = _OFF_GG + GV_W


def _bf(x):
    return x.astype(jnp.bfloat16)


def _dot(a, b):
    return jnp.dot(a, b, preferred_element_type=jnp.float32)


def _dot_nt(a, b):
    return lax.dot_general(a, b, (((1,), (1,)), ((), ())), preferred_element_type=jnp.float32)


def _sigmoid(x):
    return 1.0 / (1.0 + jnp.exp(-x))


def _silu(x):
    return x * _sigmoid(x)


def _rms(x, n=None):
    n = x.shape[-1] if n is None else n
    ss = jnp.sum(x * x, axis=-1, keepdims=True) * (1.0 / n)
    return x * lax.rsqrt(ss + EPS)


def _modulate(x, g, shift, scale):
    return _rms(x) * g * (1.0 + scale) + shift


def _cparams(sem):
    return pltpu.CompilerParams(dimension_semantics=sem, vmem_limit_bytes=VMEM_LIMIT)


def _const_spec(shape):
    nd = len(shape)
    return pl.BlockSpec(shape, lambda *_: (0,) * nd, pipeline_mode=pl.Buffered(1))


def _ada_kernel(c_ref, w_ref, b_ref, o_ref):
    s = _bf(_silu(c_ref[...]))
    o_ref[0] = _dot(s, _bf(w_ref[0])) + b_ref[0]


def _ada_call(cv, w_ada, b_ada):
    depth = w_ada.shape[0]
    tn = 1536
    return pl.pallas_call(
        _ada_kernel,
        out_shape=jax.ShapeDtypeStruct((depth, 8, 6 * D), jnp.float32),
        grid=(depth, 6 * D // tn),
        in_specs=[pl.BlockSpec((8, D), lambda i, j: (0, 0)),
                  pl.BlockSpec((1, D, tn), lambda i, j: (i, 0, j)),
                  pl.BlockSpec((1, 1, tn), lambda i, j: (i, 0, j))],
        out_specs=pl.BlockSpec((1, 8, tn), lambda i, j: (i, 0, j)),
        compiler_params=_cparams(("arbitrary", "arbitrary")),
        name="ada_params",
    )(cv, w_ada, b_ada.reshape(depth, 1, 6 * D))


def _rope_kernel(f_ref, isrow_ref, sa_ref, sb_ref, c_ref, a_ref, b_ref):
    n = c_ref.shape[0]
    t = lax.broadcasted_iota(jnp.int32, (n, SLAB), 0)
    r = (t // GRID_W).astype(jnp.float32)
    col = (t % GRID_W).astype(jnp.float32)
    pos = jnp.where(isrow_ref[...] > 0.5, r, col)
    ang = pos * f_ref[...]
    cs = jnp.cos(ang)
    sn = jnp.sin(ang)
    rope_lane = (sa_ref[...] != 0.0) | (sb_ref[...] != 0.0)
    c_ref[...] = jnp.where(rope_lane, cs, 1.0)
    a_ref[...] = sn * sa_ref[...]
    b_ref[...] = sn * sb_ref[...]


def _rope_tables(n_tokens):
    half = QK_ROPE // 2
    m = half // 2
    inv_freq = ROPE_THETA ** (-np.arange(0, half, 2, dtype=np.float32) / half)
    f = np.zeros((1, SLAB), np.float32)
    isrow = np.zeros((1, SLAB), np.float32)
    sa = np.zeros((1, SLAB), np.float32)
    sb = np.zeros((1, SLAB), np.float32)
    for axis in range(2):
        base = QK_NOPE + axis * half
        f[0, base:base + m] = inv_freq
        f[0, base + m:base + half] = inv_freq
        isrow[0, base:base + half] = 1.0 if axis == 0 else 0.0
        sa[0, base:base + m] = -1.0
        sb[0, base + m:base + half] = 1.0
    out = jax.ShapeDtypeStruct((n_tokens, SLAB), jnp.float32)
    return pl.pallas_call(_rope_kernel, out_shape=(out, out, out), name="rope_tables")(
        jnp.asarray(f), jnp.asarray(isrow), jnp.asarray(sa), jnp.asarray(sb))


def _head_norm_rope(slab, gain, rope):
    y = _rms(slab, QK_HEAD) * gain
    if rope is not None:
        c, a, b = rope
        y = y * c + pltpu.roll(y, SLAB - QK_ROPE // 4, 1) * a + pltpu.roll(y, QK_ROPE // 4, 1) * b
    return y


def _store(ref, val, lanes=slice(None)):
    if len(ref.shape) == 3:
        ref[:, :, lanes] = val.reshape(ref.shape[0], ref.shape[1], val.shape[-1])
    else:
        ref[:, lanes] = val


def _keys_values(ckv_n, krs, wukv_ref, kg_ref, rope, k_ref, v_ref):
    kv = _dot(_bf(ckv_n), wukv_ref[...])
    for h in range(MLA_HEADS):
        sl = slice(h * SLAB, (h + 1) * SLAB)
        _store(k_ref, _bf(_head_norm_rope(kv[:, sl] + krs, kg_ref[:, sl], rope)), sl)
    _store(v_ref, _bf(kv[:, QK_W:]))


def _mix_in_kernel(*refs, rope, rows):
    (x_ref, mod_ref, nrm_ref, win_ref, qan_ref, wuq_ref, kvan_ref, wukv_ref, qg_ref, kg_ref,
     wal_ref, bal_ref) = refs[:12]
    pos = 12
    rope_t = None
    if rope:
        rope_t = (refs[12][...], refs[13][...], refs[14][...])
        pos = 15
    q_ref, k_ref, v_ref, gq_ref, gk_ref, gv_ref, gg_ref, la_ref, ckv_ref, kr_ref = refs[pos:]

    x = x_ref[...].reshape(rows, D)
    m = mod_ref[0]
    h = _modulate(x, nrm_ref[...], m[0:1], m[1:2])
    proj = _dot(_bf(h), win_ref[...])
    cq = proj[:, _OFF_CQ:_OFF_CQ + Q_LORA]
    ckv = proj[:, _OFF_CKV:_OFF_CKV + KV_LORA]
    xs = proj[:, _OFF_XS:_OFF_XS + SLAB]
    krs = proj[:, _OFF_KRS:_OFF_KRS + SLAB]

    q = _dot(_bf(_rms(cq) * qan_ref[...]), wuq_ref[...])
    for hd in range(MLA_HEADS):
        sl = slice(hd * SLAB, (hd + 1) * SLAB)
        _store(q_ref, _bf(_head_norm_rope(q[:, sl], qg_ref[:, sl], rope_t)), sl)

    ckv_n = _rms(ckv) * kvan_ref[...]
    _store(ckv_ref, ckv_n)
    _store(kr_ref, xs[:, :QK_ROPE])
    _keys_values(ckv_n, krs, wukv_ref, kg_ref, rope_t, k_ref, v_ref)

    _store(gq_ref, proj[:, _OFF_GQ:_OFF_GQ + GQ_W])
    _store(gk_ref, proj[:, _OFF_GK:_OFF_GK + GQ_W])
    _store(gv_ref, proj[:, _OFF_GV:_OFF_GV + GV_W])
    _store(gg_ref, proj[:, _OFF_GG:_OFF_GG + GV_W])
    z = _dot(_bf(xs), wal_ref[...]) + bal_ref[...]
    log_sig = jnp.minimum(z, 0.0) - jnp.log1p(jnp.exp(-jnp.abs(z)))
    _store(la_ref, log_sig * (1.0 / GLA_TAU))


def _mix_in_call(x, mods, mod_row, w, rope_t, nb, tl):
    bsz, seq, _ = x.shape
    rows = nb * tl
    grid = (bsz // nb, seq // tl)
    tok = lambda width: pl.BlockSpec((nb, tl, width), lambda b, l: (b, l, 0))
    in_specs = [tok(D),
                pl.BlockSpec((1, 6, D), lambda b, l: (mod_row(b), 0, 0)),
                _const_spec((1, D)), _const_spec((D, IN_W_PAD)), _const_spec((1, Q_LORA)),
                _const_spec((Q_LORA, QK_W)), _const_spec((1, KV_LORA)),
                _const_spec((KV_LORA, QK_W + V_W)), _const_spec((1, QK_W)), _const_spec((1, QK_W)),
                _const_spec((SLAB, 2 * GQ_W)), _const_spec((1, 2 * GQ_W))]
    args = [x, mods, w["norm_mix"], w["w_in"], w["qa_norm"], w["w_uq"], w["kva_norm"], w["w_ukv"],
            w["q_gain"], w["k_gain"], w["w_alpha"], w["b_alpha"]]
    if rope_t is not None:
        in_specs += [pl.BlockSpec((tl, SLAB), lambda b, l: (l, 0))] * 3
        args += list(rope_t)
    widths = [(QK_W, jnp.bfloat16), (QK_W, jnp.bfloat16), (V_W, jnp.bfloat16), (GQ_W, jnp.float32),
              (GQ_W, jnp.float32), (GV_W, jnp.float32), (GV_W, jnp.float32), (2 * GQ_W, jnp.float32),
              (KV_LORA, jnp.float32), (QK_ROPE, jnp.float32)]
    return pl.pallas_call(
        functools.partial(_mix_in_kernel, rope=rope_t is not None, rows=rows),
        out_shape=tuple(jax.ShapeDtypeStruct((bsz, seq, wd), dt) for wd, dt in widths),
        grid=grid,
        in_specs=in_specs,
        out_specs=tuple(tok(wd) for wd, _ in widths),
        compiler_params=_cparams(("arbitrary", "arbitrary")),
        name="mix_in",
    )(*args)


def _kv_cache_kernel(ckv_ref, krs_ref, wukv_ref, kg_ref, k_ref, v_ref):
    _keys_values(ckv_ref[0], krs_ref[0], wukv_ref, kg_ref, None, k_ref, v_ref)


def _kv_cache_call(ckv_c, krs_c, w):
    bsz, past, _ = ckv_c.shape
    blk = lambda width: pl.BlockSpec((1, past, width), lambda b: (b, 0, 0))
    return pl.pallas_call(
        _kv_cache_kernel,
        out_shape=(jax.ShapeDtypeStruct((bsz, past, QK_W), jnp.bfloat16),
                   jax.ShapeDtypeStruct((bsz, past, V_W), jnp.bfloat16)),
        grid=(bsz,),
        in_specs=[blk(KV_LORA), blk(SLAB), _const_spec((KV_LORA, QK_W + V_W)), _const_spec((1, QK_W))],
        out_specs=(blk(QK_W), blk(V_W)),
        compiler_params=_cparams(("arbitrary",)),
        name="kv_cache",
    )(ckv_c, krs_c, w["w_ukv"], w["k_gain"])


def _attn_kernel(*refs, has_cache):
    if has_cache:
        q_ref, kn_ref, vn_ref, kc_ref, vc_ref, o_ref = refs
    else:
        q_ref, kn_ref, vn_ref, o_ref = refs
        kc_ref = vc_ref = None
    lane = lax.broadcasted_iota(jnp.int32, (1, 2 * V_HEAD), 1)
    acc = None
    for h in range(2):
        sl = slice(h * SLAB, (h + 1) * SLAB)
        keep = (lane < V_HEAD) if h == 0 else (lane >= V_HEAD)
        qh = q_ref[0, :, sl]
        s_n = _dot_nt(qh, kn_ref[0, :, sl])
        mx = jnp.max(s_n, axis=-1, keepdims=True)
        if has_cache:
            s_c = _dot_nt(qh, kc_ref[0, :, sl])
            mx = jnp.maximum(mx, jnp.max(s_c, axis=-1, keepdims=True))
        p_n = jnp.exp(s_n - mx)
        den = jnp.sum(p_n, axis=-1, keepdims=True)
        o = _dot(_bf(p_n), jnp.where(keep, vn_ref[0], jnp.zeros_like(vn_ref[0])))
        if has_cache:
            p_c = jnp.exp(s_c - mx)
            den = den + jnp.sum(p_c, axis=-1, keepdims=True)
            o = o + _dot(_bf(p_c), jnp.where(keep, vc_ref[0], jnp.zeros_like(vc_ref[0])))
        o = o * (1.0 / den)
        acc = o if acc is None else acc + o
    o_ref[0] = _bf(acc)


def _attn_call(q, kn, vn, cache, tq):
    bsz, seq, _ = q.shape
    lk = kn.shape[1]
    npair = MLA_HEADS // 2
    in_specs = [pl.BlockSpec((1, tq, 2 * SLAB), lambda b, p, i: (b, i, p)),
                pl.BlockSpec((1, lk, 2 * SLAB), lambda b, p, i: (b, 0, p)),
                pl.BlockSpec((1, lk, 2 * V_HEAD), lambda b, p, i: (b, 0, p))]
    args = [q, kn, vn]
    if cache is not None:
        past = cache[0].shape[1]
        in_specs += [pl.BlockSpec((1, past, 2 * SLAB), lambda b, p, i: (b, 0, p)),
                     pl.BlockSpec((1, past, 2 * V_HEAD), lambda b, p, i: (b, 0, p))]
        args += list(cache)
    return pl.pallas_call(
        functools.partial(_attn_kernel, has_cache=cache is not None),
        out_shape=jax.ShapeDtypeStruct((bsz, seq, V_W), jnp.bfloat16),
        grid=(bsz, npair, seq // tq),
        in_specs=in_specs,
        out_specs=pl.BlockSpec((1, tq, 2 * V_HEAD), lambda b, p, i: (b, i, p)),
        compiler_params=_cparams(("arbitrary", "arbitrary", "arbitrary")),
        name="mla_attention",
    )(*args)


def _gla_consts():
    t = np.arange(GLA_MC)
    same = (t[:, None] // GLA_BLK) == (t[None, :] // GLA_BLK)
    tri_f = same & (t[None, :] <= t[:, None])
    tri_b = same & (t[None, :] >= t[:, None])
    sums = np.stack([np.concatenate([tri_f, same], 0), np.concatenate([tri_b, same], 0)]).astype(np.float32)
    red = np.zeros((GLA_BLK, 2, GLA_DK, 2, GLA_NBLK, GLA_BLK), np.float32)
    for jj in range(GLA_BLK):
        for hd in range(2):
            red[jj, hd, :, hd, :, jj] = 1.0
    red = red.reshape(GLA_BLK * 2 * GLA_DK, 2 * GLA_MC)
    return jnp.asarray(sums, jnp.bfloat16), jnp.asarray(red, jnp.bfloat16)


def _split3(g):
    g1 = _bf(g)
    r1 = g - g1.astype(jnp.float32)
    g2 = _bf(r1)
    g3 = _bf(r1 - g2.astype(jnp.float32))
    return jnp.concatenate([g1, g2, g3], axis=1)


def _gla_direction(q_ref, k_ref, v_ref, la_ref, sums, red_ref, st0, o_ref, backward, accumulate):
    seq = q_ref.shape[1]
    n_mc = seq // GLA_MC
    two_dk = 2 * GLA_DK
    two_dv = 2 * GLA_DV
    row = lax.broadcasted_iota(jnp.int32, (GLA_MC, two_dv), 0)
    colv = lax.broadcasted_iota(jnp.int32, (GLA_MC, two_dv), 1)
    jj_of = colv % GLA_BLK
    blk_of = (colv // GLA_BLK) % GLA_NBLK
    in_blk = blk_of == row // GLA_BLK
    causal = (jj_of >= row % GLA_BLK) if backward else (jj_of <= row % GLA_BLK)
    a_mask = in_blk & causal
    srow = lax.broadcasted_iota(jnp.int32, (two_dv, two_dk), 0)
    slane = lax.broadcasted_iota(jnp.int32, (two_dv, two_dk), 1)
    st_mask = (srow // GLA_DV) == (slane // GLA_DK)
    krow = lax.broadcasted_iota(jnp.int32, (GLA_MC, two_dk), 0) // GLA_BLK
    zeros_v = jnp.zeros((GLA_MC, GLA_DV), jnp.bfloat16)

    def body(c, st):
        cc = (n_mc - 1 - c) if backward else c
        r0 = pl.multiple_of(cc * GLA_MC, GLA_MC)
        q = q_ref[0, pl.ds(r0, GLA_MC), :]
        k = k_ref[0, pl.ds(r0, GLA_MC), :]
        g = la_ref[0, pl.ds(r0, GLA_MC), :]
        v = v_ref[0, pl.ds(r0, GLA_MC), :]
        vb = _bf(v)

        parts = _dot(sums, _split3(g))
        parts = parts[:, :two_dk] + parts[:, two_dk:2 * two_dk] + parts[:, 2 * two_dk:]
        cum = parts[:GLA_MC]
        tot = parts[GLA_MC:]
        qh = _bf(q * jnp.exp(cum))
        kh = _bf(k * jnp.exp(tot - cum))
        dec = jnp.exp(tot)

        q3 = q.reshape(GLA_NBLK, GLA_BLK, two_dk)
        k3 = k.reshape(GLA_NBLK, GLA_BLK, two_dk)
        c3 = cum.reshape(GLA_NBLK, GLA_BLK, two_dk)
        ws = []
        for jj in range(GLA_BLK):
            kb = jnp.broadcast_to(k3[:, jj:jj + 1, :], q3.shape)
            cb = jnp.broadcast_to(c3[:, jj:jj + 1, :], q3.shape)
            e = jnp.exp(jnp.minimum(c3 - cb, 0.0))
            ws.append(_bf((q3 * kb * e).reshape(GLA_MC, two_dk)))
        a = _dot(jnp.concatenate(ws, axis=1), red_ref[...])
        a = _bf(jnp.where(a_mask, a, 0.0))
        v_bd = jnp.concatenate([jnp.concatenate([vb[:, :GLA_DV], zeros_v], axis=1),
                                jnp.concatenate([zeros_v, vb[:, GLA_DV:]], axis=1)], axis=0)
        o_intra = _dot(a, v_bd)

        v_t = _bf(jnp.transpose(v))
        o_blocks = [None] * GLA_NBLK
        order = range(GLA_NBLK - 1, -1, -1) if backward else range(GLA_NBLK)
        for blk in order:
            rs = slice(blk * GLA_BLK, (blk + 1) * GLA_BLK)
            o_blocks[blk] = o_intra[rs] + _dot_nt(qh[rs], _bf(st))
            k_sel = jnp.where(krow == blk, kh, jnp.zeros_like(kh))
            upd = _dot(v_t, k_sel)
            st = st * dec[blk * GLA_BLK:blk * GLA_BLK + 1] + jnp.where(st_mask, upd, 0.0)
        o = jnp.concatenate(o_blocks, axis=0)
        if accumulate:
            o_ref[0, pl.ds(r0, GLA_MC), :] += o
        else:
            o_ref[0, pl.ds(r0, GLA_MC), :] = o
        return st

    return lax.fori_loop(0, n_mc, body, st0)


def _gla_kernel(*refs, has_state):
    if has_state:
        q_ref, k_ref, v_ref, laf_ref, lab_ref, sums_ref, red_ref, s0_ref, o_ref, sf_ref = refs
    else:
        q_ref, k_ref, v_ref, laf_ref, lab_ref, sums_ref, red_ref, o_ref, sf_ref = refs
    z = jnp.zeros((GLA_DK, GLA_DV), jnp.float32)
    for d in range(2):
        if has_state:
            s_bd = jnp.concatenate([jnp.concatenate([s0_ref[0, d, 0], z], axis=1),
                                    jnp.concatenate([z, s0_ref[0, d, 1]], axis=1)], axis=0)
            st0 = jnp.transpose(s_bd)
        else:
            st0 = jnp.zeros((2 * GLA_DV, 2 * GLA_DK), jnp.float32)
        la_ref = laf_ref if d == 0 else lab_ref
        st = _gla_direction(q_ref, k_ref, v_ref, la_ref, sums_ref[d], red_ref, st0, o_ref,
                            backward=d == 1, accumulate=d == 1)
        s_fin = jnp.transpose(st)
        sf_ref[0, d, 0] = s_fin[:GLA_DK, :GLA_DV]
        sf_ref[0, d, 1] = s_fin[GLA_DK:, GLA_DV:]


def _gla_call(gq, gk, gv, la, s0, consts):
    bsz, seq, _ = gq.shape
    npair = GLA_HEADS // 2
    sums, red = consts
    qk_spec = pl.BlockSpec((1, seq, 2 * GLA_DK), lambda b, p: (b, 0, p))
    v_spec = pl.BlockSpec((1, seq, 2 * GLA_DV), lambda b, p: (b, 0, p))
    st_spec = pl.BlockSpec((1, 2, 2, GLA_DK, GLA_DV), lambda b, p: (b, 0, p, 0, 0))
    in_specs = [qk_spec, qk_spec, v_spec, qk_spec,
                pl.BlockSpec((1, seq, 2 * GLA_DK), lambda b, p: (b, 0, npair + p)),
                _const_spec(sums.shape), _const_spec(red.shape)]
    args = [gq, gk, gv, la, la, sums, red]
    if s0 is not None:
        in_specs.append(st_spec)
        args.append(s0)
    return pl.pallas_call(
        functools.partial(_gla_kernel, has_state=s0 is not None),
        out_shape=(jax.ShapeDtypeStruct((bsz, seq, GV_W), jnp.float32),
                   jax.ShapeDtypeStruct((bsz, 2, GLA_HEADS, GLA_DK, GLA_DV), jnp.float32)),
        grid=(bsz, npair),
        in_specs=in_specs,
        out_specs=(v_spec, st_spec),
        compiler_params=_cparams(("arbitrary", "arbitrary")),
        name="gla_bidir",
    )(*args)


def _ffn(hb, wg_ref, wu_ref, wd_ref):
    acc = None
    for c in range(FFN_NCHUNK):
        g = _dot(hb, wg_ref[c])
        u = _dot(hb, wu_ref[c])
        part = _dot(_bf(_silu(g) * u), wd_ref[c])
        acc = part if acc is None else acc + part
    return acc


def _ffn_specs():
    return [_const_spec((FFN_NCHUNK, D, FFN_CHUNK)), _const_spec((FFN_NCHUNK, D, FFN_CHUNK)),
            _const_spec((FFN_NCHUNK, FFN_CHUNK, D))]


def _mix_out_kernel(x_ref, mod_ref, attn_ref, o_ref, gg_ref, gn_ref, woa_ref, wog_ref, nrm_ref,
                    wg_ref, wu_ref, wd_ref, y_ref, *, rows):
    x = x_ref[...].reshape(rows, D)
    m = mod_ref[0]
    o = o_ref[...].reshape(rows, GV_W)
    gg = gg_ref[...].reshape(rows, GV_W)
    out = _dot(attn_ref[...].reshape(rows, V_W), woa_ref[...])
    gated = []
    for h in range(GLA_HEADS):
        sl = slice(h * GLA_DV, (h + 1) * GLA_DV)
        gated.append(_bf(_rms(o[:, sl]) * gn_ref[...] * _silu(gg[:, sl])))
    out = out + _dot(jnp.concatenate(gated, axis=1), wog_ref[...])
    x1 = x + m[2:3] * out
    h2 = _modulate(x1, nrm_ref[...], m[3:4], m[4:5])
    y = x1 + m[5:6] * _ffn(_bf(h2), wg_ref, wu_ref, wd_ref)
    y_ref[...] = y.reshape(y_ref.shape)


def _mix_out_call(x, mods, mod_row, attn, o, gg, w, nb, tl):
    bsz, seq, _ = x.shape
    rows = nb * tl
    tok = lambda width: pl.BlockSpec((nb, tl, width), lambda b, l: (b, l, 0))
    return pl.pallas_call(
        functools.partial(_mix_out_kernel, rows=rows),
        out_shape=jax.ShapeDtypeStruct(x.shape, jnp.float32),
        grid=(bsz // nb, seq // tl),
        in_specs=[tok(D), pl.BlockSpec((1, 6, D), lambda b, l: (mod_row(b), 0, 0)),
                  tok(V_W), tok(GV_W), tok(GV_W), _const_spec((1, GLA_DV)),
                  _const_spec((V_W, D)), _const_spec((GV_W, D)), _const_spec((1, D))] + _ffn_specs(),
        out_specs=tok(D),
        compiler_params=_cparams(("arbitrary", "arbitrary")),
        name="mix_out_ffn",
    )(x, mods, attn, o, gg, w["gla_norm"], w["w_out_attn"], w["w_out_gla"], w["norm_ffn"],
      w["ffn_gate"], w["ffn_up"], w["ffn_down"])


def _conv_in_kernel(x_ref, mod_ref, nrm_ref, w1_ref, b1_ref, u_ref, *, rows):
    x = x_ref[...].reshape(rows, D)
    m = mod_ref[0]
    hb = _bf(_modulate(x, nrm_ref[...], m[0:1], m[1:2]))
    a = _dot(hb, w1_ref[:, :D]) + b1_ref[:, :D]
    g = _dot(hb, w1_ref[:, D:]) + b1_ref[:, D:]
    u_ref[...] = (a * _sigmoid(g)).reshape(u_ref.shape)


def _conv_in_call(x, mods, mod_row, w, nb, tl):
    bsz, seq, _ = x.shape
    tok = pl.BlockSpec((nb, tl, D), lambda b, l: (b, l, 0))
    return pl.pallas_call(
        functools.partial(_conv_in_kernel, rows=nb * tl),
        out_shape=jax.ShapeDtypeStruct(x.shape, jnp.float32),
        grid=(bsz // nb, seq // tl),
        in_specs=[tok, pl.BlockSpec((1, 6, D), lambda b, l: (mod_row(b), 0, 0)),
                  _const_spec((1, D)), _const_spec((D, 2 * D)), _const_spec((1, 2 * D))],
        out_specs=tok,
        compiler_params=_cparams(("arbitrary", "arbitrary")),
        name="conv_in",
    )(x, mods, w["norm_mix"], w["conv_w_pw1"], w["conv_b_pw1"])


def _conv_out_kernel(*refs, rows, nb, tl, has_halo):
    if has_halo:
        (x_ref, mod_ref, u_ref, up_ref, un_ref, wdw_ref, bdw_ref, lng_ref, lnb_ref, w2_ref, b2_ref,
         nrm_ref, wg_ref, wu_ref, wd_ref, y_ref, ext_ref) = refs
    else:
        (x_ref, mod_ref, u_ref, wdw_ref, bdw_ref, lng_ref, lnb_ref, w2_ref, b2_ref,
         nrm_ref, wg_ref, wu_ref, wd_ref, y_ref, ext_ref) = refs
    zero_halo = jnp.zeros((nb, HALO, D), jnp.float32)
    ext_ref[:, HALO:HALO + tl, :] = u_ref[...]
    if has_halo:
        li = pl.program_id(1)
        first = li == 0
        last = li == pl.num_programs(1) - 1
        ext_ref[:, 0:HALO, :] = jnp.where(first, zero_halo, up_ref[...])
        ext_ref[:, HALO + tl:, :] = jnp.where(last, zero_halo, un_ref[...])
    else:
        ext_ref[:, 0:HALO, :] = zero_halo
        ext_ref[:, HALO + tl:, :] = zero_halo
    acc = None
    for tap in range(CONV_WIDTH):
        start = HALO - CONV_PAD + tap
        term = ext_ref[:, start:start + tl, :] * wdw_ref[tap:tap + 1, :]
        acc = term if acc is None else acc + term
    cv = acc.reshape(rows, D) + bdw_ref[...]
    mu = jnp.mean(cv, axis=-1, keepdims=True)
    cc = cv - mu
    var = jnp.mean(cc * cc, axis=-1, keepdims=True)
    ln = cc * lax.rsqrt(var + EPS) * lng_ref[...] + lnb_ref[...]
    out = _dot(_bf(_silu(ln)), w2_ref[...]) + b2_ref[...]
    x = x_ref[...].reshape(rows, D)
    m = mod_ref[0]
    x1 = x + m[2:3] * out
    h2 = _modulate(x1, nrm_ref[...], m[3:4], m[4:5])
    y = x1 + m[5:6] * _ffn(_bf(h2), wg_ref, wu_ref, wd_ref)
    y_ref[...] = y.reshape(y_ref.shape)


def _conv_out_call(x, mods, mod_row, u, w, nb, tl):
    bsz, seq, _ = x.shape
    has_halo = tl < seq
    assert has_halo or tl == seq
    assert not (has_halo and nb != 1)
    tok = pl.BlockSpec((nb, tl, D), lambda b, l: (b, l, 0))
    in_specs = [tok, pl.BlockSpec((1, 6, D), lambda b, l: (mod_row(b), 0, 0)), tok]
    args = [x, mods, u]
    if has_halo:
        per = tl // HALO
        nhb = seq // HALO
        in_specs += [pl.BlockSpec((1, HALO, D), lambda b, l: (b, jnp.maximum(l * per - 1, 0), 0)),
                     pl.BlockSpec((1, HALO, D), lambda b, l: (b, jnp.minimum((l + 1) * per, nhb - 1), 0))]
        args += [u, u]
    in_specs += [_const_spec((CONV_WIDTH, D)), _const_spec((1, D)), _const_spec((1, D)), _const_spec((1, D)),
                 _const_spec((D, D)), _const_spec((1, D)), _const_spec((1, D))] + _ffn_specs()
    args += [w["conv_w_dw"], w["conv_b_dw"], w["conv_ln_g"], w["conv_ln_b"], w["conv_w_pw2"],
             w["conv_b_pw2"], w["norm_ffn"], w["ffn_gate"], w["ffn_up"], w["ffn_down"]]
    return pl.pallas_call(
        functools.partial(_conv_out_kernel, rows=nb * tl, nb=nb, tl=tl, has_halo=has_halo),
        out_shape=jax.ShapeDtypeStruct(x.shape, jnp.float32),
        grid=(bsz // nb, seq // tl),
        in_specs=in_specs,
        out_specs=tok,
        scratch_shapes=[pltpu.VMEM((nb, tl + 2 * HALO, D), jnp.float32)],
        compiler_params=_cparams(("arbitrary", "arbitrary")),
        name="conv_out_ffn",
    )(*args)


def _prep_even(j, w_in, qa_norm, w_uq, kva_norm, w_ukv, q_norm, k_norm, w_alpha_up, b_alpha, gla_norm,
               w_out):
    wi = w_in[j]
    o = np.cumsum((0, Q_LORA, KV_LORA, QK_ROPE, GQ_W, GQ_W, GV_W, GV_W, 2 * ALPHA_RANK))
    cq, ckv, kr, gq, gk, gv, gg, ga = (wi[:, o[i]:o[i + 1]] for i in range(8))
    zeros = lambda n: jnp.zeros((D, n), wi.dtype)
    xs = jnp.concatenate([kr, ga, zeros(SLAB - QK_ROPE - 2 * ALPHA_RANK)], axis=1)
    krs = jnp.concatenate([zeros(QK_NOPE), kr, zeros(SLAB - QK_HEAD)], axis=1)
    w_in_p = jnp.concatenate([cq, ckv, xs, krs, gq * (GLA_DK ** -0.5), gk, gv, gg], axis=1)

    pad_head = lambda t: jnp.pad(t, [(0, 0)] * (t.ndim - 1) + [(0, SLAB - t.shape[-1])])
    w_uq_p = pad_head(w_uq[j].reshape(Q_LORA, MLA_HEADS, QK_HEAD)).reshape(Q_LORA, QK_W)
    ukv = w_ukv[j].reshape(KV_LORA, MLA_HEADS, QK_NOPE + V_HEAD)
    w_k = pad_head(ukv[:, :, :QK_NOPE]).reshape(KV_LORA, QK_W)
    w_v = ukv[:, :, QK_NOPE:].reshape(KV_LORA, V_W)
    gain = lambda g: jnp.tile(pad_head(g), MLA_HEADS).reshape(1, QK_W)

    w_al = jnp.zeros((SLAB, 2 * GQ_W), jnp.float32)
    for d in range(2):
        r0 = QK_ROPE + d * ALPHA_RANK
        w_al = w_al.at[r0:r0 + ALPHA_RANK, d * GQ_W:(d + 1) * GQ_W].set(w_alpha_up[j, d])
    return dict(
        w_in=_bf(w_in_p), qa_norm=qa_norm[j].reshape(1, -1), w_uq=_bf(w_uq_p),
        kva_norm=kva_norm[j].reshape(1, -1), w_ukv=_bf(jnp.concatenate([w_k, w_v], axis=1)),
        q_gain=gain(q_norm[j]) * (1.0 / math.sqrt(QK_HEAD)), k_gain=gain(k_norm[j]),
        w_alpha=_bf(w_al), b_alpha=b_alpha[j].reshape(1, 2 * GQ_W), gla_norm=gla_norm[j].reshape(1, -1),
        w_out_attn=_bf(w_out[j, :V_W]), w_out_gla=_bf(w_out[j, V_W:]))


def _prep_ffn(i, norm_ffn, ffn_w_gate, ffn_w_up, ffn_w_down):
    cols = lambda t: _bf(t.reshape(D, FFN_NCHUNK, FFN_CHUNK).transpose(1, 0, 2))
    return dict(norm_ffn=norm_ffn[i].reshape(1, D), ffn_gate=cols(ffn_w_gate[i]), ffn_up=cols(ffn_w_up[i]),
                ffn_down=_bf(ffn_w_down[i].reshape(FFN_NCHUNK, FFN_CHUNK, D)))


def _trunk(x, mods, mod_row, w0, w1, rope_t, cache, s0, gla_consts, nb, tl, tq):
    q, k, v, gq, gk, gv, gg, la, ckv_n, kr = _mix_in_call(x, mods[0], mod_row, w0, rope_t, nb, tl)
    attn = _attn_call(q, k, v, cache, tq)
    o, s_fin = _gla_call(gq, gk, gv, la, s0, gla_consts)
    x = _mix_out_call(x, mods[0], mod_row, attn, o, gg, w0, nb, tl)
    u = _conv_in_call(x, mods[1], mod_row, w1, nb, tl)
    x = _conv_out_call(x, mods[1], mod_row, u, w1, nb, tl)
    return x, ckv_n, kr, s_fin


def kernel(x_prompt, x_sample, cache_mla_ckv, cache_mla_krope, state_gla, c, c_ctx, w_ada, b_ada, norm_mix,
           norm_ffn, w_in, qa_norm, w_uq, kva_norm, w_ukv, q_norm, k_norm, w_alpha_up, b_alpha, gla_norm,
           w_out, conv_w_pw1, conv_b_pw1, conv_w_dw, conv_b_dw, conv_ln_g, conv_ln_b, conv_w_pw2,
           conv_b_pw2, ffn_w_gate, ffn_w_up, ffn_w_down):
    dec_batch = c.shape[0]
    ctx_row = dec_batch
    cv = jnp.concatenate([c, c_ctx[None], jnp.zeros((8 - dec_batch - 1, D), c.dtype)], axis=0)
    mods = _ada_call(cv, w_ada, b_ada).reshape(w_ada.shape[0], 8, 6, D)

    w0 = _prep_even(0, w_in, qa_norm, w_uq, kva_norm, w_ukv, q_norm, k_norm, w_alpha_up, b_alpha,
                    gla_norm, w_out)
    w0.update(_prep_ffn(0, norm_ffn, ffn_w_gate, ffn_w_up, ffn_w_down), norm_mix=norm_mix[0].reshape(1, D))
    w1 = dict(norm_mix=norm_mix[1].reshape(1, D), conv_w_pw1=_bf(conv_w_pw1[0]),
              conv_b_pw1=conv_b_pw1[0].reshape(1, -1), conv_w_dw=conv_w_dw[0],
              conv_b_dw=conv_b_dw[0].reshape(1, D), conv_ln_g=conv_ln_g[0].reshape(1, D),
              conv_ln_b=conv_ln_b[0].reshape(1, D), conv_w_pw2=_bf(conv_w_pw2[0]),
              conv_b_pw2=conv_b_pw2[0].reshape(1, D))
    w1.update(_prep_ffn(1, norm_ffn, ffn_w_gate, ffn_w_up, ffn_w_down))
    gla_consts = _gla_consts()

    seq = x_prompt.shape[1]
    y_prompt, ckv_n, kr, s_fin = _trunk(x_prompt, mods, lambda b: ctx_row, w0, w1, None, None, None,
                                        gla_consts, nb=2, tl=seq, tq=seq)

    rope_t = _rope_tables(x_sample.shape[1])
    krs_c = jnp.pad(cache_mla_krope[:, 0], ((0, 0), (0, 0), (QK_NOPE, SLAB - QK_HEAD)))
    cache = _kv_cache_call(cache_mla_ckv[:, 0], krs_c, w0)
    y_sample, _, _, _ = _trunk(x_sample, mods, lambda b: b, w0, w1, rope_t, cache, state_gla[:, 0],
                               gla_consts, nb=1, tl=512, tq=256)

    return (y_prompt, y_sample, ckv_n[:, None], kr[:, None], s_fin[:, None])
```

```python
import functools
import math

import numpy as np
import jax
import jax.numpy as jnp
from jax import lax
from jax.experimental import pallas as pl
from jax.experimental.pallas import tpu as pltpu

D = 1024
GRID_W = 64
MLA_HEADS = 8
Q_LORA = 384
KV_LORA = 256
QK_NOPE = 64
QK_ROPE = 32
QK_HEAD = QK_NOPE + QK_ROPE
V_HEAD = 64
ROPE_THETA = 10000.0
GLA_HEADS = 4
GLA_DK = 64
GLA_DV = 128
ALPHA_RANK = 16
GLA_TAU = 16.0
CONV_WIDTH = 31
CONV_PAD = CONV_WIDTH // 2
FFN_HIDDEN = 2816
EPS = 1e-6

LANES = 128
MXU_COLS = 256
VMEM_LIMIT = 56 * 1024 * 1024

SLAB = LANES
QK_W = MLA_HEADS * SLAB
V_W = MLA_HEADS * V_HEAD
GQ_W = GLA_HEADS * GLA_DK
GV_W = GLA_HEADS * GLA_DV
FFN_CHUNK = MXU_COLS
FFN_NCHUNK = FFN_HIDDEN // FFN_CHUNK
HALO = 16
GLA_BLK = 16
GLA_MC = 128
GLA_NBLK = GLA_MC // GLA_BLK
CONV_ROWS = 64
ATTN_KCHUNK = 2 * MXU_COLS

_OFF_CQ = 0
_OFF_CKV = _OFF_CQ + Q_LORA
_OFF_XS = _OFF_CKV + KV_LORA
_OFF_KRS = _OFF_XS + SLAB
_OFF_GQ = _OFF_KRS + SLAB
_OFF_GK = _OFF_GQ + GQ_W
_OFF_GV = _OFF_GK + GQ_W
_OFF_GG = _OFF_GV + GV_W
IN_W_PAD = _OFF_GG + GV_W


def _bf(x):
    return x.astype(jnp.bfloat16)


def _dot(a, b):
    return jnp.dot(a, b, preferred_element_type=jnp.float32)


def _dot_nt(a, b):
    return lax.dot_general(a, b, (((1,), (1,)), ((), ())), preferred_element_type=jnp.float32)


def _dot_tn(a, b):
    return lax.dot_general(a, b, (((0,), (0,)), ((), ())), preferred_element_type=jnp.float32)


def _sigmoid(x):
    return 1.0 / (1.0 + jnp.exp(-x))


def _silu(x):
    return x * _sigmoid(x)


def _rms(x, n=None):
    n = x.shape[-1] if n is None else n
    ss = jnp.sum(x * x, axis=-1, keepdims=True) * (1.0 / n)
    return x * lax.rsqrt(ss + EPS)


def _modulate(x, g, shift, scale):
    return _rms(x) * g * (1.0 + scale) + shift


def _cparams(sem):
    return pltpu.CompilerParams(dimension_semantics=sem, vmem_limit_bytes=VMEM_LIMIT)


def _const_spec(shape):
    nd = len(shape)
    return pl.BlockSpec(shape, lambda *_: (0,) * nd, pipeline_mode=pl.Buffered(1))


def _ada_kernel(c_ref, w_ref, b_ref, o_ref):
    s = _bf(_silu(c_ref[...]))
    o_ref[0] = _dot(s, _bf(w_ref[0])) + b_ref[0]


def _ada_call(cv, w_ada, b_ada):
    depth = w_ada.shape[0]
    tn = 1536
    return pl.pallas_call(
        _ada_kernel,
        out_shape=jax.ShapeDtypeStruct((depth, 8, 6 * D), jnp.float32),
        grid=(depth, 6 * D // tn),
        in_specs=[pl.BlockSpec((8, D), lambda i, j: (0, 0)),
                  pl.BlockSpec((1, D, tn), lambda i, j: (i, 0, j)),
                  pl.BlockSpec((1, 1, tn), lambda i, j: (i, 0, j))],
        out_specs=pl.BlockSpec((1, 8, tn), lambda i, j: (i, 0, j)),
        compiler_params=_cparams(("arbitrary", "arbitrary")),
        name="ada_params",
    )(cv, w_ada, b_ada.reshape(depth, 1, 6 * D))


def _rope_kernel(f_ref, isrow_ref, sa_ref, sb_ref, c_ref, s_ref):
    n = c_ref.shape[0]
    t = lax.broadcasted_iota(jnp.int32, (n, SLAB), 0)
    r = (t // GRID_W).astype(jnp.float32)
    col = (t % GRID_W).astype(jnp.float32)
    pos = jnp.where(isrow_ref[...] > 0.5, r, col)
    ang = pos * f_ref[...]
    cs = jnp.cos(ang)
    sn = jnp.sin(ang)
    rope_lane = (sa_ref[...] != 0.0) | (sb_ref[...] != 0.0)
    c_ref[...] = jnp.where(rope_lane, cs, 1.0)
    s_ref[...] = sn * (sa_ref[...] + sb_ref[...])


def _rope_tables(n_tokens):
    half = QK_ROPE // 2
    m = half // 2
    inv_freq = ROPE_THETA ** (-np.arange(0, half, 2, dtype=np.float32) / half)
    f = np.zeros((1, SLAB), np.float32)
    isrow = np.zeros((1, SLAB), np.float32)
    sa = np.zeros((1, SLAB), np.float32)
    sb = np.zeros((1, SLAB), np.float32)
    for axis in range(2):
        base = QK_NOPE + axis * half
        f[0, base:base + m] = inv_freq
        f[0, base + m:base + half] = inv_freq
        isrow[0, base:base + half] = 1.0 if axis == 0 else 0.0
        sa[0, base:base + m] = -1.0
        sb[0, base + m:base + half] = 1.0
    out = jax.ShapeDtypeStruct((n_tokens, SLAB), jnp.float32)
    partner = np.zeros((SLAB, SLAB), np.float32)
    for dst in range(SLAB):
        if sa[0, dst] != 0.0:
            partner[dst + m, dst] = 1.0
        if sb[0, dst] != 0.0:
            partner[dst - m, dst] = 1.0
    c_tab, s_tab = pl.pallas_call(_rope_kernel, out_shape=(out, out), name="rope_tables")(
        jnp.asarray(f), jnp.asarray(isrow), jnp.asarray(sa), jnp.asarray(sb))
    return c_tab, s_tab, jnp.asarray(partner, jnp.bfloat16)


def _head_norm_rope(slab, gain, rope):
    y = _rms(slab, QK_HEAD) * gain
    if rope is not None:
        c, s, partner = rope
        y = y * c + _dot(_bf(y), partner) * s
    return y


def _store(ref, val, lanes=slice(None)):
    if len(ref.shape) == 3:
        ref[:, :, lanes] = val.reshape(ref.shape[0], ref.shape[1], val.shape[-1])
    else:
        ref[:, lanes] = val


def _keys_values(ckv_n, krs, wukv_ref, kg_ref, rope, k_ref, v_ref):
    kv = _dot(_bf(ckv_n), wukv_ref[...])
    for h in range(MLA_HEADS):
        sl = slice(h * SLAB, (h + 1) * SLAB)
        _store(k_ref, _bf(_head_norm_rope(kv[:, sl] + krs, kg_ref[:, sl], rope)), sl)
    _store(v_ref, _bf(kv[:, QK_W:]))


def _mix_in_kernel(*refs, rope, rows):
    (x_ref, mod_ref, nrm_ref, win_ref, qan_ref, wuq_ref, kvan_ref, wukv_ref, qg_ref, kg_ref,
     wal_ref, bal_ref) = refs[:12]
    pos = 12
    rope_t = None
    if rope:
        rope_t = (refs[12][...], refs[13][...], refs[14][...])
        pos = 15
    q_ref, k_ref, v_ref, gq_ref, gk_ref, gv_ref, gg_ref, la_ref, ckv_ref, kr_ref = refs[pos:]

    x = x_ref[...].reshape(rows, D)
    m = mod_ref[0]
    h = _modulate(x, nrm_ref[...], m[0:1], m[1:2])
    proj = _dot(_bf(h), win_ref[...])
    cq = proj[:, _OFF_CQ:_OFF_CQ + Q_LORA]
    ckv = proj[:, _OFF_CKV:_OFF_CKV + KV_LORA]
    xs = proj[:, _OFF_XS:_OFF_XS + SLAB]
    krs = proj[:, _OFF_KRS:_OFF_KRS + SLAB]

    q = _dot(_bf(_rms(cq) * qan_ref[...]), wuq_ref[...])
    for hd in range(MLA_HEADS):
        sl = slice(hd * SLAB, (hd + 1) * SLAB)
        _store(q_ref, _bf(_head_norm_rope(q[:, sl], qg_ref[:, sl], rope_t)), sl)

    ckv_n = _rms(ckv) * kvan_ref[...]
    _store(ckv_ref, ckv_n)
    _store(kr_ref, xs[:, :QK_ROPE])
    _keys_values(ckv_n, krs, wukv_ref, kg_ref, rope_t, k_ref, v_ref)

    _store(gq_ref, proj[:, _OFF_GQ:_OFF_GQ + GQ_W])
    _store(gk_ref, proj[:, _OFF_GK:_OFF_GK + GQ_W])
    _store(gv_ref, proj[:, _OFF_GV:_OFF_GV + GV_W])
    _store(gg_ref, proj[:, _OFF_GG:_OFF_GG + GV_W])
    z = _dot(_bf(xs), wal_ref[...]) + bal_ref[...]
    log_sig = jnp.minimum(z, 0.0) - jnp.log1p(jnp.exp(-jnp.abs(z)))
    _store(la_ref, log_sig * (1.0 / GLA_TAU))


def _mix_in_call(x, mods, mod_row, w, rope_t, nb, tl):
    bsz, seq, _ = x.shape
    rows = nb * tl
    grid = (bsz // nb, seq // tl)
    tok = lambda width: pl.BlockSpec((nb, tl, width), lambda b, l: (b, l, 0))
    in_specs = [tok(D),
                pl.BlockSpec((1, 6, D), lambda b, l: (mod_row(b), 0, 0)),
                _const_spec((1, D)), _const_spec((D, IN_W_PAD)), _const_spec((1, Q_LORA)),
                _const_spec((Q_LORA, QK_W)), _const_spec((1, KV_LORA)),
                _const_spec((KV_LORA, QK_W + V_W)), _const_spec((1, QK_W)), _const_spec((1, QK_W)),
                _const_spec((SLAB, 2 * GQ_W)), _const_spec((1, 2 * GQ_W))]
    args = [x, mods, w["norm_mix"], w["w_in"], w["qa_norm"], w["w_uq"], w["kva_norm"], w["w_ukv"],
            w["q_gain"], w["k_gain"], w["w_alpha"], w["b_alpha"]]
    if rope_t is not None:
        assert nb == 1
        in_specs += [pl.BlockSpec((tl, SLAB), lambda b, l: (l, 0))] * 2 + [_const_spec((SLAB, SLAB))]
        args += list(rope_t)
    widths = [(QK_W, jnp.bfloat16), (QK_W, jnp.bfloat16), (V_W, jnp.bfloat16), (GQ_W, jnp.float32),
              (GQ_W, jnp.float32), (GV_W, jnp.float32), (GV_W, jnp.float32), (2 * GQ_W, jnp.float32),
              (KV_LORA, jnp.float32), (QK_ROPE, jnp.float32)]
    return pl.pallas_call(
        functools.partial(_mix_in_kernel, rope=rope_t is not None, rows=rows),
        out_shape=tuple(jax.ShapeDtypeStruct((bsz, seq, wd), dt) for wd, dt in widths),
        grid=grid,
        in_specs=in_specs,
        out_specs=tuple(tok(wd) for wd, _ in widths),
        compiler_params=_cparams(("arbitrary", "arbitrary")),
        name="mix_in",
    )(*args)


def _kv_cache_kernel(ckv_ref, krs_ref, wukv_ref, kg_ref, k_ref, v_ref):
    _keys_values(ckv_ref[0], krs_ref[0], wukv_ref, kg_ref, None, k_ref, v_ref)


def _kv_cache_call(ckv_c, krs_c, w):
    bsz, past, _ = ckv_c.shape
    blk = lambda width: pl.BlockSpec((1, past, width), lambda b: (b, 0, 0))
    return pl.pallas_call(
        _kv_cache_kernel,
        out_shape=(jax.ShapeDtypeStruct((bsz, past, QK_W), jnp.bfloat16),
                   jax.ShapeDtypeStruct((bsz, past, V_W), jnp.bfloat16)),
        grid=(bsz,),
        in_specs=[blk(KV_LORA), blk(SLAB), _const_spec((KV_LORA, QK_W + V_W)), _const_spec((1, QK_W))],
        out_specs=(blk(QK_W), blk(V_W)),
        compiler_params=_cparams(("arbitrary",)),
        name="kv_cache",
    )(ckv_c, krs_c, w["w_ukv"], w["k_gain"])


def _attn_kernel(*refs, has_cache, nba):
    if has_cache:
        q_ref, kn_ref, vn_ref, kc_ref, vc_ref, o_ref, va_ref = refs
    else:
        q_ref, kn_ref, vn_ref, o_ref, va_ref = refs
        kc_ref = vc_ref = None
    lk = kn_ref.shape[1]
    lane = lax.broadcasted_iota(jnp.int32, (1, 2 * V_HEAD), 1)
    lo = lane < V_HEAD

    @pl.when(pl.program_id(2) == 0)
    def _():
        for b in range(nba):
            for h in range(2):
                keep = lo if h == 0 else jnp.logical_not(lo)
                vn = vn_ref[b]
                va_ref[b, h, 0:lk, :] = jnp.where(keep, vn, jnp.ones_like(vn))
                if has_cache:
                    vc = vc_ref[b]
                    va_ref[b, h, lk:, :] = jnp.where(keep, vc, jnp.ones_like(vc))

    chunks = [(kn_ref, c0, c0, min(ATTN_KCHUNK, lk - c0)) for c0 in range(0, lk, ATTN_KCHUNK)]
    if has_cache:
        chunks.append((kc_ref, 0, lk, kc_ref.shape[1]))
    for b in range(nba):
        scores = []
        for h in range(2):
            sl = slice(h * SLAB, (h + 1) * SLAB)
            qh = q_ref[b, :, sl]
            parts = []
            m_el = None
            for k_ref, r0, _, n in chunks:
                s = _dot_nt(qh, k_ref[b, r0:r0 + n, sl])
                parts.append(s)
                for c in range(0, n, LANES):
                    m_el = s[:, c:c + LANES] if m_el is None else jnp.maximum(m_el, s[:, c:c + LANES])
            scores.append((parts, jnp.max(m_el, axis=-1, keepdims=True)))
        outs = []
        for h in range(2):
            parts, mx = scores[h]
            o = None
            for s, (_, _, v0, n) in zip(parts, chunks):
                term = _dot(_bf(jnp.exp2(s - mx)), va_ref[b, h, v0:v0 + n, :])
                o = term if o is None else o + term
            outs.append(o / pltpu.roll(o, V_HEAD, 1))
        o_ref[b] = _bf(jnp.where(lo, outs[0], outs[1]))


def _attn_call(q, kn, vn, cache, tq, nba):
    bsz, seq, _ = q.shape
    lk = kn.shape[1]
    npair = MLA_HEADS // 2
    in_specs = [pl.BlockSpec((nba, tq, 2 * SLAB), lambda b, p, i: (b, i, p)),
                pl.BlockSpec((nba, lk, 2 * SLAB), lambda b, p, i: (b, 0, p)),
                pl.BlockSpec((nba, lk, 2 * V_HEAD), lambda b, p, i: (b, 0, p))]
    args = [q, kn, vn]
    past = 0
    if cache is not None:
        past = cache[0].shape[1]
        in_specs += [pl.BlockSpec((nba, past, 2 * SLAB), lambda b, p, i: (b, 0, p)),
                     pl.BlockSpec((nba, past, 2 * V_HEAD), lambda b, p, i: (b, 0, p))]
        args += list(cache)
    return pl.pallas_call(
        functools.partial(_attn_kernel, has_cache=cache is not None, nba=nba),
        out_shape=jax.ShapeDtypeStruct((bsz, seq, V_W), jnp.bfloat16),
        grid=(bsz // nba, npair, seq // tq),
        in_specs=in_specs,
        out_specs=pl.BlockSpec((nba, tq, 2 * V_HEAD), lambda b, p, i: (b, i, p)),
        scratch_shapes=[pltpu.VMEM((nba, 2, lk + past, 2 * V_HEAD), jnp.bfloat16)],
        compiler_params=_cparams(("arbitrary", "arbitrary", "arbitrary")),
        name="mla_attention",
    )(*args)


def _gla_consts():
    t = np.arange(GLA_MC)
    same = (t[:, None] // GLA_BLK) == (t[None, :] // GLA_BLK)
    tri_f = same & (t[None, :] <= t[:, None])
    tri_b = same & (t[None, :] >= t[:, None])
    sums = np.stack([np.concatenate([tri_f, same], 0), np.concatenate([tri_b, same], 0)]).astype(np.float32)
    red = np.zeros((GLA_BLK, 2, GLA_DK, 2, GLA_NBLK, GLA_BLK), np.float32)
    for jj in range(GLA_BLK):
        for hd in range(2):
            red[jj, hd, :, hd, :, jj] = 1.0
    red = red.reshape(GLA_BLK * 2 * GLA_DK, 2 * GLA_MC)
    return jnp.asarray(sums, jnp.bfloat16), jnp.asarray(red, jnp.bfloat16)


def _split3(g):
    g1 = _bf(g)
    r1 = g - g1.astype(jnp.float32)
    g2 = _bf(r1)
    g3 = _bf(r1 - g2.astype(jnp.float32))
    return jnp.concatenate([g1, g2, g3], axis=1)


def _gla_mask(backward):
    row = lax.broadcasted_iota(jnp.int32, (GLA_MC, 2 * GLA_MC), 0)
    colv = lax.broadcasted_iota(jnp.int32, (GLA_MC, 2 * GLA_MC), 1)
    jj_of = colv % GLA_BLK
    in_blk = (colv // GLA_BLK) % GLA_NBLK == row // GLA_BLK
    causal = (jj_of >= row % GLA_BLK) if backward else (jj_of <= row % GLA_BLK)
    return in_blk & causal


def _gla_chunk(q, k, g, v, sums, red_ref, st, a_mask, backward):
    two_dk = 2 * GLA_DK
    vb = _bf(v)
    parts = _dot(sums, _split3(g))
    parts = parts[:, :two_dk] + parts[:, two_dk:2 * two_dk] + parts[:, 2 * two_dk:]
    cum = parts[:GLA_MC]
    tot = parts[GLA_MC:]
    qh = _bf(q * jnp.exp(cum))
    kh = k * jnp.exp(tot - cum)
    head0 = lax.broadcasted_iota(jnp.int32, (1, two_dk), 1) < GLA_DK
    kh_heads = (_bf(jnp.where(head0, kh, 0.0)), _bf(jnp.where(head0, 0.0, kh)))
    dec = jnp.exp(tot)

    upds = []
    for blk in range(GLA_NBLK):
        rs = slice(blk * GLA_BLK, (blk + 1) * GLA_BLK)
        upds.append(jnp.concatenate(
            [_dot_tn(vb[rs, hd * GLA_DV:(hd + 1) * GLA_DV], kh_heads[hd][rs]) for hd in range(2)], axis=0))

    q3 = q.reshape(GLA_NBLK, GLA_BLK, two_dk)
    k3 = k.reshape(GLA_NBLK, GLA_BLK, two_dk)
    c3 = cum.reshape(GLA_NBLK, GLA_BLK, two_dk)
    ws = []
    for jj in range(GLA_BLK):
        kb = jnp.broadcast_to(k3[:, jj:jj + 1, :], q3.shape)
        cb = jnp.broadcast_to(c3[:, jj:jj + 1, :], q3.shape)
        e = jnp.exp(jnp.minimum(c3 - cb, 0.0))
        ws.append(_bf((q3 * kb * e).reshape(GLA_MC, two_dk)))
    a = _dot(jnp.concatenate(ws, axis=1), red_ref[...])
    a = _bf(jnp.where(a_mask, a, 0.0))
    zeros_v = jnp.zeros((GLA_MC, GLA_DV), jnp.bfloat16)
    v_bd = jnp.concatenate([jnp.concatenate([vb[:, :GLA_DV], zeros_v], axis=1),
                            jnp.concatenate([zeros_v, vb[:, GLA_DV:]], axis=1)], axis=0)
    o_intra = _dot(a, v_bd)

    o_blocks = [None] * GLA_NBLK
    order = range(GLA_NBLK - 1, -1, -1) if backward else range(GLA_NBLK)
    for blk in order:
        rs = slice(blk * GLA_BLK, (blk + 1) * GLA_BLK)
        o_blocks[blk] = o_intra[rs] + _dot_nt(qh[rs], _bf(st))
        st = st * dec[blk * GLA_BLK:blk * GLA_BLK + 1] + upds[blk]
    return jnp.concatenate(o_blocks, axis=0), st


def _gla_kernel(*refs, has_state):
    if has_state:
        q_ref, k_ref, v_ref, laf_ref, lab_ref, sums_ref, red_ref, s0_ref, o_ref, sf_ref = refs
    else:
        q_ref, k_ref, v_ref, laf_ref, lab_ref, sums_ref, red_ref, o_ref, sf_ref = refs
    n_mc = q_ref.shape[1] // GLA_MC
    z = jnp.zeros((GLA_DK, GLA_DV), jnp.float32)
    st0 = []
    for d in range(2):
        if has_state:
            s_bd = jnp.concatenate([jnp.concatenate([s0_ref[0, d, 0], z], axis=1),
                                    jnp.concatenate([z, s0_ref[0, d, 1]], axis=1)], axis=0)
            st0.append(jnp.transpose(s_bd))
        else:
            st0.append(jnp.zeros((2 * GLA_DV, 2 * GLA_DK), jnp.float32))
    masks = (_gla_mask(False), _gla_mask(True))
    o_ref[...] = jnp.zeros(o_ref.shape, o_ref.dtype)

    def body(c, carry):
        new = []
        for d in range(2):
            r0 = pl.multiple_of(((n_mc - 1 - c) if d else c) * GLA_MC, GLA_MC)
            rows = pl.ds(r0, GLA_MC)
            la_ref = lab_ref if d else laf_ref
            o, st = _gla_chunk(q_ref[0, rows, :], k_ref[0, rows, :], la_ref[0, rows, :], v_ref[0, rows, :],
                               sums_ref[d], red_ref, carry[d], masks[d], backward=bool(d))
            o_ref[0, rows, :] += o
            new.append(st)
        return tuple(new)

    finals = lax.fori_loop(0, n_mc, body, tuple(st0))
    for d in range(2):
        s_fin = jnp.transpose(finals[d])
        sf_ref[0, d, 0] = s_fin[:GLA_DK, :GLA_DV]
        sf_ref[0, d, 1] = s_fin[GLA_DK:, GLA_DV:]


def _gla_call(gq, gk, gv, la, s0, consts):
    bsz, seq, _ = gq.shape
    npair = GLA_HEADS // 2
    sums, red = consts
    qk_spec = pl.BlockSpec((1, seq, 2 * GLA_DK), lambda b, p: (b, 0, p))
    v_spec = pl.BlockSpec((1, seq, 2 * GLA_DV), lambda b, p: (b, 0, p))
    st_spec = pl.BlockSpec((1, 2, 2, GLA_DK, GLA_DV), lambda b, p: (b, 0, p, 0, 0))
    in_specs = [qk_spec, qk_spec, v_spec, qk_spec,
                pl.BlockSpec((1, seq, 2 * GLA_DK), lambda b, p: (b, 0, npair + p)),
                _const_spec(sums.shape), _const_spec(red.shape)]
    args = [gq, gk, gv, la, la, sums, red]
    if s0 is not None:
        in_specs.append(st_spec)
        args.append(s0)
    return pl.pallas_call(
        functools.partial(_gla_kernel, has_state=s0 is not None),
        out_shape=(jax.ShapeDtypeStruct((bsz, seq, GV_W), jnp.float32),
                   jax.ShapeDtypeStruct((bsz, 2, GLA_HEADS, GLA_DK, GLA_DV), jnp.float32)),
        grid=(bsz, npair),
        in_specs=in_specs,
        out_specs=(v_spec, st_spec),
        compiler_params=_cparams(("arbitrary", "arbitrary")),
        name="gla_bidir",
    )(*args)


def _ffn(hb, wg_ref, wu_ref, wd_ref):
    acc = None
    for c in range(FFN_NCHUNK):
        g = _dot(hb, wg_ref[c])
        u = _dot(hb, wu_ref[c])
        part = _dot(_bf(_silu(g) * u), wd_ref[c])
        acc = part if acc is None else acc + part
    return acc


def _ffn_specs():
    return [_const_spec((FFN_NCHUNK, D, FFN_CHUNK)), _const_spec((FFN_NCHUNK, D, FFN_CHUNK)),
            _const_spec((FFN_NCHUNK, FFN_CHUNK, D))]


def _mix_out_kernel(x_ref, mod_ref, attn_ref, o_ref, gg_ref, gn_ref, woa_ref, wog_ref, nrm_ref,
                    wg_ref, wu_ref, wd_ref, y_ref, *, rows):
    x = x_ref[...].reshape(rows, D)
    m = mod_ref[0]
    o = o_ref[...].reshape(rows, GV_W)
    gg = gg_ref[...].reshape(rows, GV_W)
    out = _dot(attn_ref[...].reshape(rows, V_W), woa_ref[...])
    gated = []
    for h in range(GLA_HEADS):
        sl = slice(h * GLA_DV, (h + 1) * GLA_DV)
        gated.append(_bf(_rms(o[:, sl]) * gn_ref[...] * _silu(gg[:, sl])))
    out = out + _dot(jnp.concatenate(gated, axis=1), wog_ref[...])
    x1 = x + m[2:3] * out
    h2 = _modulate(x1, nrm_ref[...], m[3:4], m[4:5])
    y = x1 + m[5:6] * _ffn(_bf(h2), wg_ref, wu_ref, wd_ref)
    y_ref[...] = y.reshape(y_ref.shape)


def _mix_out_call(x, mods, mod_row, attn, o, gg, w, nb, tl):
    bsz, seq, _ = x.shape
    rows = nb * tl
    tok = lambda width: pl.BlockSpec((nb, tl, width), lambda b, l: (b, l, 0))
    return pl.pallas_call(
        functools.partial(_mix_out_kernel, rows=rows),
        out_shape=jax.ShapeDtypeStruct(x.shape, jnp.float32),
        grid=(bsz // nb, seq // tl),
        in_specs=[tok(D), pl.BlockSpec((1, 6, D), lambda b, l: (mod_row(b), 0, 0)),
                  tok(V_W), tok(GV_W), tok(GV_W), _const_spec((1, GLA_DV)),
                  _const_spec((V_W, D)), _const_spec((GV_W, D)), _const_spec((1, D))] + _ffn_specs(),
        out_specs=tok(D),
        compiler_params=_cparams(("arbitrary", "arbitrary")),
        name="mix_out_ffn",
    )(x, mods, attn, o, gg, w["gla_norm"], w["w_out_attn"], w["w_out_gla"], w["norm_ffn"],
      w["ffn_gate"], w["ffn_up"], w["ffn_down"])


def _conv_in_kernel(x_ref, mod_ref, nrm_ref, w1_ref, b1_ref, u_ref, *, rows):
    x = x_ref[...].reshape(rows, D)
    m = mod_ref[0]
    hb = _bf(_modulate(x, nrm_ref[...], m[0:1], m[1:2]))
    a = _dot(hb, w1_ref[:, :D]) + b1_ref[:, :D]
    g = _dot(hb, w1_ref[:, D:]) + b1_ref[:, D:]
    u_ref[...] = (a * _sigmoid(g)).reshape(u_ref.shape)


def _conv_in_call(x, mods, mod_row, w, nb, tl):
    bsz, seq, _ = x.shape
    tok = pl.BlockSpec((nb, tl, D), lambda b, l: (b, l, 0))
    return pl.pallas_call(
        functools.partial(_conv_in_kernel, rows=nb * tl),
        out_shape=jax.ShapeDtypeStruct(x.shape, jnp.float32),
        grid=(bsz // nb, seq // tl),
        in_specs=[tok, pl.BlockSpec((1, 6, D), lambda b, l: (mod_row(b), 0, 0)),
                  _const_spec((1, D)), _const_spec((D, 2 * D)), _const_spec((1, 2 * D))],
        out_specs=tok,
        compiler_params=_cparams(("arbitrary", "arbitrary")),
        name="conv_in",
    )(x, mods, w["norm_mix"], w["conv_w_pw1"], w["conv_b_pw1"])


def _conv_out_kernel(*refs, rows, nb, tl, has_halo):
    if has_halo:
        (x_ref, mod_ref, u_ref, up_ref, un_ref, wdw_ref, bdw_ref, lng_ref, lnb_ref, w2_ref, b2_ref,
         nrm_ref, wg_ref, wu_ref, wd_ref, y_ref, ext_ref, act_ref) = refs
    else:
        (x_ref, mod_ref, u_ref, wdw_ref, bdw_ref, lng_ref, lnb_ref, w2_ref, b2_ref,
         nrm_ref, wg_ref, wu_ref, wd_ref, y_ref, ext_ref, act_ref) = refs
    zero_halo = jnp.zeros((nb, HALO, D), jnp.float32)
    ext_ref[:, HALO:HALO + tl, :] = u_ref[...]
    if has_halo:
        li = pl.program_id(1)
        first = li == 0
        last = li == pl.num_programs(1) - 1
        ext_ref[:, 0:HALO, :] = jnp.where(first, zero_halo, up_ref[...])
        ext_ref[:, HALO + tl:, :] = jnp.where(last, zero_halo, un_ref[...])
    else:
        ext_ref[:, 0:HALO, :] = zero_halo
        ext_ref[:, HALO + tl:, :] = zero_halo
    win = CONV_ROWS + 2 * HALO
    for b in range(nb):
        for c0 in range(0, tl, CONV_ROWS):
            tiles = []
            for lt in range(D // LANES):
                ls = slice(lt * LANES, (lt + 1) * LANES)
                xw = ext_ref[b, c0:c0 + win, ls]
                acc = None
                for r in range(8):
                    shifted = xw if r == 0 else pltpu.roll(xw, win - r, 0)
                    for a in range(win // 8):
                        tap = 8 * a + r - (HALO - CONV_PAD)
                        if 0 <= tap < CONV_WIDTH:
                            term = shifted[8 * a:8 * a + CONV_ROWS] * wdw_ref[tap:tap + 1, ls]
                            acc = term if acc is None else acc + term
                tiles.append(acc)
            cv = jnp.concatenate(tiles, axis=1) + bdw_ref[...]
            mu = jnp.mean(cv, axis=-1, keepdims=True)
            cc = cv - mu
            var = jnp.mean(cc * cc, axis=-1, keepdims=True)
            ln = cc * lax.rsqrt(var + EPS) * lng_ref[...] + lnb_ref[...]
            act_ref[b * tl + c0:b * tl + c0 + CONV_ROWS, :] = _bf(_silu(ln))
    out = _dot(act_ref[...], w2_ref[...]) + b2_ref[...]
    x = x_ref[...].reshape(rows, D)
    m = mod_ref[0]
    x1 = x + m[2:3] * out
    h2 = _modulate(x1, nrm_ref[...], m[3:4], m[4:5])
    y = x1 + m[5:6] * _ffn(_bf(h2), wg_ref, wu_ref, wd_ref)
    y_ref[...] = y.reshape(y_ref.shape)


def _conv_out_call(x, mods, mod_row, u, w, nb, tl):
    bsz, seq, _ = x.shape
    has_halo = tl < seq
    assert has_halo or tl == seq
    assert not (has_halo and nb != 1)
    tok = pl.BlockSpec((nb, tl, D), lambda b, l: (b, l, 0))
    in_specs = [tok, pl.BlockSpec((1, 6, D), lambda b, l: (mod_row(b), 0, 0)), tok]
    args = [x, mods, u]
    if has_halo:
        per = tl // HALO
        nhb = seq // HALO
        in_specs += [pl.BlockSpec((1, HALO, D), lambda b, l: (b, jnp.maximum(l * per - 1, 0), 0)),
                     pl.BlockSpec((1, HALO, D), lambda b, l: (b, jnp.minimum((l + 1) * per, nhb - 1), 0))]
        args += [u, u]
    in_specs += [_const_spec((CONV_WIDTH, D)), _const_spec((1, D)), _const_spec((1, D)), _const_spec((1, D)),
                 _const_spec((D, D)), _const_spec((1, D)), _const_spec((1, D))] + _ffn_specs()
    args += [w["conv_w_dw"], w["conv_b_dw"], w["conv_ln_g"], w["conv_ln_b"], w["conv_w_pw2"],
             w["conv_b_pw2"], w["norm_ffn"], w["ffn_gate"], w["ffn_up"], w["ffn_down"]]
    return pl.pallas_call(
        functools.partial(_conv_out_kernel, rows=nb * tl, nb=nb, tl=tl, has_halo=has_halo),
        out_shape=jax.ShapeDtypeStruct(x.shape, jnp.float32),
        grid=(bsz // nb, seq // tl),
        in_specs=in_specs,
        out_specs=tok,
        scratch_shapes=[pltpu.VMEM((nb, tl + 2 * HALO, D), jnp.float32),
                        pltpu.VMEM((nb * tl, D), jnp.bfloat16)],
        compiler_params=_cparams(("arbitrary", "arbitrary")),
        name="conv_out_ffn",
    )(*args)


def _prep_even(j, w_in, qa_norm, w_uq, kva_norm, w_ukv, q_norm, k_norm, w_alpha_up, b_alpha, gla_norm,
               w_out):
    wi = w_in[j]
    o = np.cumsum((0, Q_LORA, KV_LORA, QK_ROPE, GQ_W, GQ_W, GV_W, GV_W, 2 * ALPHA_RANK))
    cq, ckv, kr, gq, gk, gv, gg, ga = (wi[:, o[i]:o[i + 1]] for i in range(8))
    zeros = lambda n: jnp.zeros((D, n), wi.dtype)
    xs = jnp.concatenate([kr, ga, zeros(SLAB - QK_ROPE - 2 * ALPHA_RANK)], axis=1)
    krs = jnp.concatenate([zeros(QK_NOPE), kr, zeros(SLAB - QK_HEAD)], axis=1)
    w_in_p = jnp.concatenate([cq, ckv, xs, krs, gq * (GLA_DK ** -0.5), gk, gv, gg], axis=1)

    pad_head = lambda t: jnp.pad(t, [(0, 0)] * (t.ndim - 1) + [(0, SLAB - t.shape[-1])])
    w_uq_p = pad_head(w_uq[j].reshape(Q_LORA, MLA_HEADS, QK_HEAD)).reshape(Q_LORA, QK_W)
    ukv = w_ukv[j].reshape(KV_LORA, MLA_HEADS, QK_NOPE + V_HEAD)
    w_k = pad_head(ukv[:, :, :QK_NOPE]).reshape(KV_LORA, QK_W)
    w_v = ukv[:, :, QK_NOPE:].reshape(KV_LORA, V_W)
    gain = lambda g: jnp.tile(pad_head(g), MLA_HEADS).reshape(1, QK_W)

    w_al = jnp.zeros((SLAB, 2 * GQ_W), jnp.float32)
    for d in range(2):
        r0 = QK_ROPE + d * ALPHA_RANK
        w_al = w_al.at[r0:r0 + ALPHA_RANK, d * GQ_W:(d + 1) * GQ_W].set(w_alpha_up[j, d])
    return dict(
        w_in=_bf(w_in_p), qa_norm=qa_norm[j].reshape(1, -1), w_uq=_bf(w_uq_p),
        kva_norm=kva_norm[j].reshape(1, -1), w_ukv=_bf(jnp.concatenate([w_k, w_v], axis=1)),
        q_gain=gain(q_norm[j]) * (math.log2(math.e) / math.sqrt(QK_HEAD)), k_gain=gain(k_norm[j]),
        w_alpha=_bf(w_al), b_alpha=b_alpha[j].reshape(1, 2 * GQ_W), gla_norm=gla_norm[j].reshape(1, -1),
        w_out_attn=_bf(w_out[j, :V_W]), w_out_gla=_bf(w_out[j, V_W:]))


def _prep_ffn(i, norm_ffn, ffn_w_gate, ffn_w_up, ffn_w_down):
    cols = lambda t: _bf(t.reshape(D, FFN_NCHUNK, FFN_CHUNK).transpose(1, 0, 2))
    return dict(norm_ffn=norm_ffn[i].reshape(1, D), ffn_gate=cols(ffn_w_gate[i]), ffn_up=cols(ffn_w_up[i]),
                ffn_down=_bf(ffn_w_down[i].reshape(FFN_NCHUNK, FFN_CHUNK, D)))


def _trunk(x, mods, mod_row, w0, w1, rope_t, cache, s0, gla_consts, nb, tl, tq, nba):
    q, k, v, gq, gk, gv, gg, la, ckv_n, kr = _mix_in_call(x, mods[0], mod_row, w0, rope_t, nb, tl)
    attn = _attn_call(q, k, v, cache, tq, nba)
    o, s_fin = _gla_call(gq, gk, gv, la, s0, gla_consts)
    x = _mix_out_call(x, mods[0], mod_row, attn, o, gg, w0, nb, tl)
    u = _conv_in_call(x, mods[1], mod_row, w1, nb, tl)
    x = _conv_out_call(x, mods[1], mod_row, u, w1, nb, tl)
    return x, ckv_n, kr, s_fin


def kernel(x_prompt, x_sample, cache_mla_ckv, cache_mla_krope, state_gla, c, c_ctx, w_ada, b_ada, norm_mix,
           norm_ffn, w_in, qa_norm, w_uq, kva_norm, w_ukv, q_norm, k_norm, w_alpha_up, b_alpha, gla_norm,
           w_out, conv_w_pw1, conv_b_pw1, conv_w_dw, conv_b_dw, conv_ln_g, conv_ln_b, conv_w_pw2,
           conv_b_pw2, ffn_w_gate, ffn_w_up, ffn_w_down):
    dec_batch = c.shape[0]
    ctx_row = dec_batch
    cv = jnp.concatenate([c, c_ctx[None], jnp.zeros((8 - dec_batch - 1, D), c.dtype)], axis=0)
    mods = _ada_call(cv, w_ada, b_ada).reshape(w_ada.shape[0], 8, 6, D)

    w0 = _prep_even(0, w_in, qa_norm, w_uq, kva_norm, w_ukv, q_norm, k_norm, w_alpha_up, b_alpha,
                    gla_norm, w_out)
    w0.update(_prep_ffn(0, norm_ffn, ffn_w_gate, ffn_w_up, ffn_w_down), norm_mix=norm_mix[0].reshape(1, D))
    w1 = dict(norm_mix=norm_mix[1].reshape(1, D), conv_w_pw1=_bf(conv_w_pw1[0]),
              conv_b_pw1=conv_b_pw1[0].reshape(1, -1), conv_w_dw=conv_w_dw[0],
              conv_b_dw=conv_b_dw[0].reshape(1, D), conv_ln_g=conv_ln_g[0].reshape(1, D),
              conv_ln_b=conv_ln_b[0].reshape(1, D), conv_w_pw2=_bf(conv_w_pw2[0]),
              conv_b_pw2=conv_b_pw2[0].reshape(1, D))
    w1.update(_prep_ffn(1, norm_ffn, ffn_w_gate, ffn_w_up, ffn_w_down))
    gla_consts = _gla_consts()

    seq = x_prompt.shape[1]
    y_prompt, ckv_n, kr, s_fin = _trunk(x_prompt, mods, lambda b: ctx_row, w0, w1, None, None, None,
                                        gla_consts, nb=2, tl=seq, tq=seq, nba=4)

    rope_t = _rope_tables(x_sample.shape[1])
    krs_c = jnp.pad(cache_mla_krope[:, 0], ((0, 0), (0, 0), (QK_NOPE, SLAB - QK_HEAD)))
    cache = _kv_cache_call(cache_mla_ckv[:, 0], krs_c, w0)
    y_sample, _, _, _ = _trunk(x_sample, mods, lambda b: b, w0, w1, rope_t, cache, state_gla[:, 0],
                               gla_consts, nb=1, tl=512, tq=256, nba=1)

    return (y_prompt, y_sample, ckv_n[:, None], kr[:, None], s_fin[:, None])
```

```python
import functools
import math

import numpy as np
import jax
import jax.numpy as jnp
from jax import lax
from jax.experimental import pallas as pl
from jax.experimental.pallas import tpu as pltpu

D = 1024
GRID_W = 64
MLA_HEADS = 8
Q_LORA = 384
KV_LORA = 256
QK_NOPE = 64
QK_ROPE = 32
QK_HEAD = QK_NOPE + QK_ROPE
V_HEAD = 64
ROPE_THETA = 10000.0
GLA_HEADS = 4
GLA_DK = 64
GLA_DV = 128
ALPHA_RANK = 16
GLA_TAU = 16.0
CONV_WIDTH = 31
CONV_PAD = CONV_WIDTH // 2
FFN_HIDDEN = 2816
EPS = 1e-6

LANES = 128
MXU_COLS = 256
VMEM_LIMIT = 56 * 1024 * 1024

SLAB = LANES
QK_W = MLA_HEADS * SLAB
V_W = MLA_HEADS * V_HEAD
GQ_W = GLA_HEADS * GLA_DK
GV_W = GLA_HEADS * GLA_DV
FFN_CHUNK = MXU_COLS
FFN_NCHUNK = FFN_HIDDEN // FFN_CHUNK
HALO = 16
GLA_BLK = 16
GLA_MC = 128
GLA_NBLK = GLA_MC // GLA_BLK
CONV_ROWS = 64
ATTN_KCHUNK = 2 * MXU_COLS

_OFF_CQ = 0
_OFF_CKV = _OFF_CQ + Q_LORA
_OFF_XS = _OFF_CKV + KV_LORA
_OFF_KRS = _OFF_XS + SLAB
_OFF_GQ = _OFF_KRS + SLAB
_OFF_GK = _OFF_GQ + GQ_W
_OFF_GV = _OFF_GK + GQ_W
_OFF_GG = _OFF_GV + GV_W
IN_W_PAD = _OFF_GG + GV_W


def _bf(x):
    return x.astype(jnp.bfloat16)


def _dot(a, b):
    return jnp.dot(a, b, preferred_element_type=jnp.float32)


def _dot_nt(a, b):
    return lax.dot_general(a, b, (((1,), (1,)), ((), ())), preferred_element_type=jnp.float32)


def _dot_tn(a, b):
    return lax.dot_general(a, b, (((0,), (0,)), ((), ())), preferred_element_type=jnp.float32)


def _sigmoid(x):
    return 1.0 / (1.0 + jnp.exp(-x))


def _silu(x):
    return x * _sigmoid(x)


def _rms(x, n=None):
    n = x.shape[-1] if n is None else n
    ss = jnp.sum(x * x, axis=-1, keepdims=True) * (1.0 / n)
    return x * lax.rsqrt(ss + EPS)


def _modulate(x, g, shift, scale):
    return _rms(x) * g * (1.0 + scale) + shift


def _cparams(sem):
    return pltpu.CompilerParams(dimension_semantics=sem, vmem_limit_bytes=VMEM_LIMIT)


def _const_spec(shape):
    nd = len(shape)
    return pl.BlockSpec(shape, lambda *_: (0,) * nd, pipeline_mode=pl.Buffered(1))


def _ada_kernel(c_ref, w_ref, b_ref, o_ref):
    s = _bf(_silu(c_ref[...]))
    o_ref[0] = _dot(s, _bf(w_ref[0])) + b_ref[0]


def _ada_call(cv, w_ada, b_ada):
    depth = w_ada.shape[0]
    tn = 1536
    return pl.pallas_call(
        _ada_kernel,
        out_shape=jax.ShapeDtypeStruct((depth, 8, 6 * D), jnp.float32),
        grid=(depth, 6 * D // tn),
        in_specs=[pl.BlockSpec((8, D), lambda i, j: (0, 0)),
                  pl.BlockSpec((1, D, tn), lambda i, j: (i, 0, j)),
                  pl.BlockSpec((1, 1, tn), lambda i, j: (i, 0, j))],
        out_specs=pl.BlockSpec((1, 8, tn), lambda i, j: (i, 0, j)),
        compiler_params=_cparams(("arbitrary", "arbitrary")),
        name="ada_params",
    )(cv, w_ada, b_ada.reshape(depth, 1, 6 * D))


def _rope_kernel(f_ref, isrow_ref, sa_ref, sb_ref, c_ref, s_ref):
    n = c_ref.shape[0]
    t = lax.broadcasted_iota(jnp.int32, (n, SLAB), 0)
    r = (t // GRID_W).astype(jnp.float32)
    col = (t % GRID_W).astype(jnp.float32)
    pos = jnp.where(isrow_ref[...] > 0.5, r, col)
    ang = pos * f_ref[...]
    cs = jnp.cos(ang)
    sn = jnp.sin(ang)
    rope_lane = (sa_ref[...] != 0.0) | (sb_ref[...] != 0.0)
    c_ref[...] = jnp.where(rope_lane, cs, 1.0)
    s_ref[...] = sn * (sa_ref[...] + sb_ref[...])


def _rope_tables(n_tokens):
    half = QK_ROPE // 2
    m = half // 2
    inv_freq = ROPE_THETA ** (-np.arange(0, half, 2, dtype=np.float32) / half)
    f = np.zeros((1, SLAB), np.float32)
    isrow = np.zeros((1, SLAB), np.float32)
    sa = np.zeros((1, SLAB), np.float32)
    sb = np.zeros((1, SLAB), np.float32)
    for axis in range(2):
        base = QK_NOPE + axis * half
        f[0, base:base + m] = inv_freq
        f[0, base + m:base + half] = inv_freq
        isrow[0, base:base + half] = 1.0 if axis == 0 else 0.0
        sa[0, base:base + m] = -1.0
        sb[0, base + m:base + half] = 1.0
    out = jax.ShapeDtypeStruct((n_tokens, SLAB), jnp.float32)
    partner = np.zeros((SLAB, SLAB), np.float32)
    for dst in range(SLAB):
        if sa[0, dst] != 0.0:
            partner[dst + m, dst] = 1.0
        if sb[0, dst] != 0.0:
            partner[dst - m, dst] = 1.0
    c_tab, s_tab = pl.pallas_call(_rope_kernel, out_shape=(out, out), name="rope_tables")(
        jnp.asarray(f), jnp.asarray(isrow), jnp.asarray(sa), jnp.asarray(sb))
    return c_tab, s_tab, jnp.asarray(partner, jnp.bfloat16)


def _head_norm_rope(slab, gain, rope):
    y = _rms(slab, QK_HEAD) * gain
    if rope is not None:
        c, s, partner = rope
        y = y * c + _dot(_bf(y), partner) * s
    return y


def _store(ref, val, lanes=slice(None)):
    if len(ref.shape) == 3:
        ref[:, :, lanes] = val.reshape(ref.shape[0], ref.shape[1], val.shape[-1])
    else:
        ref[:, lanes] = val


def _keys_values(ckv_n, krs, wukv_ref, kg_ref, rope, k_ref, v_ref):
    kv = _dot(_bf(ckv_n), wukv_ref[...])
    for h in range(MLA_HEADS):
        sl = slice(h * SLAB, (h + 1) * SLAB)
        _store(k_ref, _bf(_head_norm_rope(kv[:, sl] + krs, kg_ref[:, sl], rope)), sl)
    _store(v_ref, _bf(kv[:, QK_W:]))


def _mix_in_kernel(*refs, rope, rows):
    (x_ref, mod_ref, nrm_ref, win_ref, qan_ref, wuq_ref, kvan_ref, wukv_ref, qg_ref, kg_ref,
     wal_ref, bal_ref) = refs[:12]
    pos = 12
    rope_t = None
    if rope:
        rope_t = (refs[12][...], refs[13][...], refs[14][...])
        pos = 15
    q_ref, k_ref, v_ref, gq_ref, gk_ref, gv_ref, gg_ref, la_ref, ckv_ref, kr_ref = refs[pos:]

    x = x_ref[...].reshape(rows, D)
    m = mod_ref[0]
    h = _modulate(x, nrm_ref[...], m[0:1], m[1:2])
    proj = _dot(_bf(h), win_ref[...])
    cq = proj[:, _OFF_CQ:_OFF_CQ + Q_LORA]
    ckv = proj[:, _OFF_CKV:_OFF_CKV + KV_LORA]
    xs = proj[:, _OFF_XS:_OFF_XS + SLAB]
    krs = proj[:, _OFF_KRS:_OFF_KRS + SLAB]

    q = _dot(_bf(_rms(cq) * qan_ref[...]), wuq_ref[...])
    for hd in range(MLA_HEADS):
        sl = slice(hd * SLAB, (hd + 1) * SLAB)
        _store(q_ref, _bf(_head_norm_rope(q[:, sl], qg_ref[:, sl], rope_t)), sl)

    ckv_n = _rms(ckv) * kvan_ref[...]
    _store(ckv_ref, ckv_n)
    _store(kr_ref, xs[:, :QK_ROPE])
    _keys_values(ckv_n, krs, wukv_ref, kg_ref, rope_t, k_ref, v_ref)

    _store(gq_ref, _bf(proj[:, _OFF_GQ:_OFF_GQ + GQ_W]))
    _store(gk_ref, _bf(proj[:, _OFF_GK:_OFF_GK + GQ_W]))
    _store(gv_ref, _bf(proj[:, _OFF_GV:_OFF_GV + GV_W]))
    _store(gg_ref, proj[:, _OFF_GG:_OFF_GG + GV_W])
    z = _dot(_bf(xs), wal_ref[...]) + bal_ref[...]
    log_sig = jnp.minimum(z, 0.0) - jnp.log1p(jnp.exp(-jnp.abs(z)))
    _store(la_ref, log_sig * (1.0 / GLA_TAU))


def _mix_in_call(x, mods, mod_row, w, rope_t, nb, tl):
    bsz, seq, _ = x.shape
    assert bsz % nb == 0 and seq % tl == 0
    rows = nb * tl
    grid = (bsz // nb, seq // tl)
    tok = lambda width: pl.BlockSpec((nb, tl, width), lambda b, l: (b, l, 0))
    in_specs = [tok(D),
                pl.BlockSpec((1, 6, D), lambda b, l: (mod_row(b), 0, 0)),
                _const_spec((1, D)), _const_spec((D, IN_W_PAD)), _const_spec((1, Q_LORA)),
                _const_spec((Q_LORA, QK_W)), _const_spec((1, KV_LORA)),
                _const_spec((KV_LORA, QK_W + V_W)), _const_spec((1, QK_W)), _const_spec((1, QK_W)),
                _const_spec((SLAB, 2 * GQ_W)), _const_spec((1, 2 * GQ_W))]
    args = [x, mods, w["norm_mix"], w["w_in"], w["qa_norm"], w["w_uq"], w["kva_norm"], w["w_ukv"],
            w["q_gain"], w["k_gain"], w["w_alpha"], w["b_alpha"]]
    if rope_t is not None:
        assert nb == 1
        in_specs += [pl.BlockSpec((tl, SLAB), lambda b, l: (l, 0))] * 2 + [_const_spec((SLAB, SLAB))]
        args += list(rope_t)
    widths = [(QK_W, jnp.bfloat16), (QK_W, jnp.bfloat16), (V_W, jnp.bfloat16), (GQ_W, jnp.bfloat16),
              (GQ_W, jnp.bfloat16), (GV_W, jnp.bfloat16), (GV_W, jnp.float32), (2 * GQ_W, jnp.float32),
              (KV_LORA, jnp.float32), (QK_ROPE, jnp.float32)]
    return pl.pallas_call(
        functools.partial(_mix_in_kernel, rope=rope_t is not None, rows=rows),
        out_shape=tuple(jax.ShapeDtypeStruct((bsz, seq, wd), dt) for wd, dt in widths),
        grid=grid,
        in_specs=in_specs,
        out_specs=tuple(tok(wd) for wd, _ in widths),
        compiler_params=_cparams(("arbitrary", "arbitrary")),
        name="mix_in",
    )(*args)


def _kv_cache_kernel(ckv_ref, krs_ref, wukv_ref, kg_ref, k_ref, v_ref):
    _keys_values(ckv_ref[0], krs_ref[0], wukv_ref, kg_ref, None, k_ref, v_ref)


def _kv_cache_call(ckv_c, krs_c, w):
    bsz, past, _ = ckv_c.shape
    blk = lambda width: pl.BlockSpec((1, past, width), lambda b: (b, 0, 0))
    return pl.pallas_call(
        _kv_cache_kernel,
        out_shape=(jax.ShapeDtypeStruct((bsz, past, QK_W), jnp.bfloat16),
                   jax.ShapeDtypeStruct((bsz, past, V_W), jnp.bfloat16)),
        grid=(bsz,),
        in_specs=[blk(KV_LORA), blk(SLAB), _const_spec((KV_LORA, QK_W + V_W)), _const_spec((1, QK_W))],
        out_specs=(blk(QK_W), blk(V_W)),
        compiler_params=_cparams(("arbitrary",)),
        name="kv_cache",
    )(ckv_c, krs_c, w["w_ukv"], w["k_gain"])


def _attn_kernel(*refs, has_cache, nba):
    if has_cache:
        q_ref, kn_ref, vn_ref, kc_ref, vc_ref, o_ref, va_ref = refs
    else:
        q_ref, kn_ref, vn_ref, o_ref, va_ref = refs
        kc_ref = vc_ref = None
    lk = kn_ref.shape[1]
    lane = lax.broadcasted_iota(jnp.int32, (1, 2 * V_HEAD), 1)
    lo = lane < V_HEAD

    @pl.when(pl.program_id(2) == 0)
    def _():
        for b in range(nba):
            for h in range(2):
                keep = lo if h == 0 else jnp.logical_not(lo)
                vn = vn_ref[b]
                va_ref[b, h, 0:lk, :] = jnp.where(keep, vn, jnp.ones_like(vn))
                if has_cache:
                    vc = vc_ref[b]
                    va_ref[b, h, lk:, :] = jnp.where(keep, vc, jnp.ones_like(vc))

    chunks = [(kn_ref, c0, c0, min(ATTN_KCHUNK, lk - c0)) for c0 in range(0, lk, ATTN_KCHUNK)]
    if has_cache:
        chunks.append((kc_ref, 0, lk, kc_ref.shape[1]))
    for b in range(nba):
        scores = []
        for h in range(2):
            sl = slice(h * SLAB, (h + 1) * SLAB)
            qh = q_ref[b, :, sl]
            parts = []
            m_el = None
            for k_ref, r0, _, n in chunks:
                s = _dot_nt(qh, k_ref[b, r0:r0 + n, sl])
                parts.append(s)
                for c in range(0, n, LANES):
                    m_el = s[:, c:c + LANES] if m_el is None else jnp.maximum(m_el, s[:, c:c + LANES])
            scores.append((parts, jnp.max(m_el, axis=-1, keepdims=True)))
        outs = []
        for h in range(2):
            parts, mx = scores[h]
            o = None
            for s, (_, _, v0, n) in zip(parts, chunks):
                term = _dot(_bf(jnp.exp2(s - mx)), va_ref[b, h, v0:v0 + n, :])
                o = term if o is None else o + term
            outs.append(o / pltpu.roll(o, V_HEAD, 1))
        o_ref[b] = _bf(jnp.where(lo, outs[0], outs[1]))


def _attn_call(q, kn, vn, cache, tq, nba):
    bsz, seq, _ = q.shape
    assert bsz % nba == 0 and seq % tq == 0
    lk = kn.shape[1]
    npair = MLA_HEADS // 2
    in_specs = [pl.BlockSpec((nba, tq, 2 * SLAB), lambda b, p, i: (b, i, p)),
                pl.BlockSpec((nba, lk, 2 * SLAB), lambda b, p, i: (b, 0, p)),
                pl.BlockSpec((nba, lk, 2 * V_HEAD), lambda b, p, i: (b, 0, p))]
    args = [q, kn, vn]
    past = 0
    if cache is not None:
        past = cache[0].shape[1]
        in_specs += [pl.BlockSpec((nba, past, 2 * SLAB), lambda b, p, i: (b, 0, p)),
                     pl.BlockSpec((nba, past, 2 * V_HEAD), lambda b, p, i: (b, 0, p))]
        args += list(cache)
    return pl.pallas_call(
        functools.partial(_attn_kernel, has_cache=cache is not None, nba=nba),
        out_shape=jax.ShapeDtypeStruct((bsz, seq, V_W), jnp.bfloat16),
        grid=(bsz // nba, npair, seq // tq),
        in_specs=in_specs,
        out_specs=pl.BlockSpec((nba, tq, 2 * V_HEAD), lambda b, p, i: (b, i, p)),
        scratch_shapes=[pltpu.VMEM((nba, 2, lk + past, 2 * V_HEAD), jnp.bfloat16)],
        compiler_params=_cparams(("arbitrary", "arbitrary", "arbitrary")),
        name="mla_attention",
    )(*args)


def _gla_consts():
    t = np.arange(GLA_MC)
    same = (t[:, None] // GLA_BLK) == (t[None, :] // GLA_BLK)
    tri_f = same & (t[None, :] <= t[:, None])
    tri_b = same & (t[None, :] >= t[:, None])
    sums = np.stack([np.concatenate([tri_f, same], 0), np.concatenate([tri_b, same], 0)]).astype(np.float32)
    red = np.zeros((GLA_BLK, 2, GLA_DK, 2, GLA_NBLK, GLA_BLK), np.float32)
    for jj in range(GLA_BLK):
        for hd in range(2):
            red[jj, hd, :, hd, :, jj] = 1.0
    red = red.reshape(GLA_BLK * 2 * GLA_DK, 2 * GLA_MC)
    return jnp.asarray(sums, jnp.bfloat16), jnp.asarray(red, jnp.bfloat16)


def _split3(g):
    g1 = _bf(g)
    r1 = g - g1.astype(jnp.float32)
    g2 = _bf(r1)
    g3 = _bf(r1 - g2.astype(jnp.float32))
    return jnp.concatenate([g1, g2, g3], axis=1)


def _gla_mask(backward):
    row = lax.broadcasted_iota(jnp.int32, (GLA_MC, 2 * GLA_MC), 0)
    colv = lax.broadcasted_iota(jnp.int32, (GLA_MC, 2 * GLA_MC), 1)
    jj_of = colv % GLA_BLK
    in_blk = (colv // GLA_BLK) % GLA_NBLK == row // GLA_BLK
    causal = (jj_of >= row % GLA_BLK) if backward else (jj_of <= row % GLA_BLK)
    return in_blk & causal


def _gla_chunks(chains, sums_ref, red_ref, masks):
    n = len(chains)
    two_dk = 2 * GLA_DK
    dirs = [ch[5] for ch in chains]
    vb = [_bf(ch[3]) for ch in chains]

    parts = [_dot(sums_ref[ch[5]], _split3(ch[2])) for ch in chains]
    parts = [p[:, :two_dk] + p[:, two_dk:2 * two_dk] + p[:, 2 * two_dk:] for p in parts]
    cum = [p[:GLA_MC] for p in parts]
    tot = [p[GLA_MC:] for p in parts]

    qh = [_bf(chains[i][0] * jnp.exp(cum[i])) for i in range(n)]
    kh_t = [_bf(jnp.transpose(chains[i][1] * jnp.exp(tot[i] - cum[i]))) for i in range(n)]
    dec_t = [jnp.transpose(jnp.exp(tot[i])) for i in range(n)]
    blk_lane = lax.broadcasted_iota(jnp.int32, (1, GLA_MC), 1) // GLA_BLK

    upds = [[None] * GLA_NBLK for _ in range(n)]
    for blk in range(GLA_NBLK):
        for i in range(n):
            k_sel = jnp.where(blk_lane == blk, kh_t[i], jnp.zeros_like(kh_t[i]))
            upds[i][blk] = [_dot(k_sel[hd * GLA_DK:(hd + 1) * GLA_DK], vb[i][:, hd * GLA_DV:(hd + 1) * GLA_DV])
                            for hd in range(2)]

    ws = [[] for _ in range(n)]
    for jj in range(GLA_BLK):
        for i in range(n):
            q3 = chains[i][0].reshape(GLA_NBLK, GLA_BLK, two_dk)
            k3 = chains[i][1].reshape(GLA_NBLK, GLA_BLK, two_dk)
            c3 = cum[i].reshape(GLA_NBLK, GLA_BLK, two_dk)
            kb = jnp.broadcast_to(k3[:, jj:jj + 1, :], q3.shape)
            cb = jnp.broadcast_to(c3[:, jj:jj + 1, :], q3.shape)
            e = jnp.exp(jnp.minimum(c3 - cb, 0.0))
            ws[i].append(_bf((q3 * kb * e).reshape(GLA_MC, two_dk)))
    zeros_v = jnp.zeros((GLA_MC, GLA_DV), jnp.bfloat16)
    o_intra = []
    for i in range(n):
        a = _dot(jnp.concatenate(ws[i], axis=1), red_ref[...])
        a = _bf(jnp.where(masks[dirs[i]], a, 0.0))
        v_bd = jnp.concatenate([jnp.concatenate([vb[i][:, :GLA_DV], zeros_v], axis=1),
                                jnp.concatenate([zeros_v, vb[i][:, GLA_DV:]], axis=1)], axis=0)
        o_intra.append(_dot(a, v_bd))

    o_blocks = [[None] * GLA_NBLK for _ in range(n)]
    zeros_s = jnp.zeros((GLA_DK, GLA_DV), jnp.bfloat16)
    st = [list(ch[4]) for ch in chains]
    for step in range(GLA_NBLK):
        for i in range(n):
            blk = GLA_NBLK - 1 - step if dirs[i] else step
            rs = slice(blk * GLA_BLK, (blk + 1) * GLA_BLK)
            s_bd = jnp.concatenate([jnp.concatenate([_bf(st[i][0]), zeros_s], axis=1),
                                    jnp.concatenate([zeros_s, _bf(st[i][1])], axis=1)], axis=0)
            o_blocks[i][blk] = o_intra[i][rs] + _dot(qh[i][rs], s_bd)
            col = blk * GLA_BLK
            for hd in range(2):
                dcol = dec_t[i][hd * GLA_DK:(hd + 1) * GLA_DK, col:col + 1]
                st[i][hd] = st[i][hd] * dcol + upds[i][blk][hd]
    return [jnp.concatenate(ob, axis=0) for ob in o_blocks], [tuple(s) for s in st]


def _gla_kernel(*refs, has_state, nbg):
    if has_state:
        q_ref, k_ref, v_ref, la_ref, sums_ref, red_ref, s0_ref, o_ref, sf_ref = refs
    else:
        q_ref, k_ref, v_ref, la_ref, sums_ref, red_ref, o_ref, sf_ref = refs
    n_mc = q_ref.shape[1] // GLA_MC
    npair = GLA_HEADS // 2
    two_dk = 2 * GLA_DK
    two_dv = 2 * GLA_DV
    z = jnp.zeros((GLA_DK, GLA_DV), jnp.float32)
    chains = [(bi, hp, d) for bi in range(nbg) for hp in range(npair) for d in range(2)]
    st0 = tuple(tuple(s0_ref[bi, d, 2 * hp + hd] if has_state else z for hd in range(2))
                for bi, hp, d in chains)
    masks = (_gla_mask(False), _gla_mask(True))
    o_ref[...] = jnp.zeros(o_ref.shape, o_ref.dtype)

    def body(c, carry):
        work, where = [], []
        for (bi, hp, d), st in zip(chains, carry):
            r0 = pl.multiple_of(((n_mc - 1 - c) if d else c) * GLA_MC, GLA_MC)
            rows = pl.ds(r0, GLA_MC)
            qk_l = slice(hp * two_dk, (hp + 1) * two_dk)
            v_l = slice(hp * two_dv, (hp + 1) * two_dv)
            la_l = slice(d * GQ_W + hp * two_dk, d * GQ_W + (hp + 1) * two_dk)
            work.append((q_ref[bi, rows, qk_l].astype(jnp.float32), k_ref[bi, rows, qk_l].astype(jnp.float32),
                         la_ref[bi, rows, la_l], v_ref[bi, rows, v_l], st, d))
            where.append((bi, rows, v_l))
        outs, new = _gla_chunks(work, sums_ref, red_ref, masks)
        for (bi, rows, v_l), o in zip(where, outs):
            o_ref[bi, rows, v_l] += o
        return tuple(new)

    finals = lax.fori_loop(0, n_mc, body, st0)
    for (bi, hp, d), st in zip(chains, finals):
        for hd in range(2):
            sf_ref[bi, d, 2 * hp + hd] = st[hd]


def _gla_call(gq, gk, gv, la, s0, consts, nbg, single_buffer):
    bsz, seq, _ = gq.shape
    assert bsz % nbg == 0 and seq % GLA_MC == 0
    sums, red = consts
    mode = dict(pipeline_mode=pl.Buffered(1)) if single_buffer else {}
    tok = lambda width: pl.BlockSpec((nbg, seq, width), lambda b: (b, 0, 0), **mode)
    st_spec = pl.BlockSpec((nbg, 2, GLA_HEADS, GLA_DK, GLA_DV), lambda b: (b, 0, 0, 0, 0))
    in_specs = [tok(GQ_W), tok(GQ_W), tok(GV_W), tok(2 * GQ_W), _const_spec(sums.shape), _const_spec(red.shape)]
    args = [gq, gk, gv, la, sums, red]
    if s0 is not None:
        in_specs.append(st_spec)
        args.append(s0)
    return pl.pallas_call(
        functools.partial(_gla_kernel, has_state=s0 is not None, nbg=nbg),
        out_shape=(jax.ShapeDtypeStruct((bsz, seq, GV_W), jnp.float32),
                   jax.ShapeDtypeStruct((bsz, 2, GLA_HEADS, GLA_DK, GLA_DV), jnp.float32)),
        grid=(bsz // nbg,),
        in_specs=in_specs,
        out_specs=(pl.BlockSpec((nbg, seq, GV_W), lambda b: (b, 0, 0)), st_spec),
        compiler_params=_cparams(("arbitrary",)),
        name="gla_bidir",
    )(*args)


def _ffn(hb, wg_ref, wu_ref, wd_ref):
    acc = None
    for c in range(FFN_NCHUNK):
        hs = slice(c * FFN_CHUNK, (c + 1) * FFN_CHUNK)
        g = _dot(hb, wg_ref[:, hs])
        u = _dot(hb, wu_ref[:, hs])
        part = _dot(_bf(_silu(g) * u), wd_ref[hs, :])
        acc = part if acc is None else acc + part
    return acc


def _ffn_specs():
    return [_const_spec((D, FFN_HIDDEN)), _const_spec((D, FFN_HIDDEN)), _const_spec((FFN_HIDDEN, D))]


def _mix_out_kernel(x_ref, mod_ref, attn_ref, o_ref, gg_ref, gn_ref, woa_ref, wog_ref, nrm_ref,
                    wg_ref, wu_ref, wd_ref, y_ref, *, rows):
    x = x_ref[...].reshape(rows, D)
    m = mod_ref[0]
    o = o_ref[...].reshape(rows, GV_W)
    gg = gg_ref[...].reshape(rows, GV_W)
    out = _dot(attn_ref[...].reshape(rows, V_W), woa_ref[...])
    gated = []
    for h in range(GLA_HEADS):
        sl = slice(h * GLA_DV, (h + 1) * GLA_DV)
        gated.append(_bf(_rms(o[:, sl]) * gn_ref[...] * _silu(gg[:, sl])))
    out = out + _dot(jnp.concatenate(gated, axis=1), wog_ref[...])
    x1 = x + m[2:3] * out
    h2 = _modulate(x1, nrm_ref[...], m[3:4], m[4:5])
    y = x1 + m[5:6] * _ffn(_bf(h2), wg_ref, wu_ref, wd_ref)
    y_ref[...] = y.reshape(y_ref.shape)


def _mix_out_call(x, mods, mod_row, attn, o, gg, w, nb, tl):
    bsz, seq, _ = x.shape
    assert bsz % nb == 0 and seq % tl == 0
    rows = nb * tl
    tok = lambda width: pl.BlockSpec((nb, tl, width), lambda b, l: (b, l, 0))
    return pl.pallas_call(
        functools.partial(_mix_out_kernel, rows=rows),
        out_shape=jax.ShapeDtypeStruct(x.shape, jnp.float32),
        grid=(bsz // nb, seq // tl),
        in_specs=[tok(D), pl.BlockSpec((1, 6, D), lambda b, l: (mod_row(b), 0, 0)),
                  tok(V_W), tok(GV_W), tok(GV_W), _const_spec((1, GLA_DV)),
                  _const_spec((V_W, D)), _const_spec((GV_W, D)), _const_spec((1, D))] + _ffn_specs(),
        out_specs=tok(D),
        compiler_params=_cparams(("arbitrary", "arbitrary")),
        name="mix_out_ffn",
    )(x, mods, attn, o, gg, w["gla_norm"], w["w_out_attn"], w["w_out_gla"], w["norm_ffn"],
      w["ffn_gate"], w["ffn_up"], w["ffn_down"])


def _conv_in_kernel(x_ref, mod_ref, nrm_ref, w1_ref, b1_ref, u_ref, *, rows):
    x = x_ref[...].reshape(rows, D)
    m = mod_ref[0]
    hb = _bf(_modulate(x, nrm_ref[...], m[0:1], m[1:2]))
    a = _dot(hb, w1_ref[:, :D]) + b1_ref[:, :D]
    g = _dot(hb, w1_ref[:, D:]) + b1_ref[:, D:]
    u_ref[...] = (a * _sigmoid(g)).reshape(u_ref.shape)


def _conv_in_call(x, mods, mod_row, w, nb, tl):
    bsz, seq, _ = x.shape
    assert bsz % nb == 0 and seq % tl == 0
    tok = pl.BlockSpec((nb, tl, D), lambda b, l: (b, l, 0))
    return pl.pallas_call(
        functools.partial(_conv_in_kernel, rows=nb * tl),
        out_shape=jax.ShapeDtypeStruct(x.shape, jnp.float32),
        grid=(bsz // nb, seq // tl),
        in_specs=[tok, pl.BlockSpec((1, 6, D), lambda b, l: (mod_row(b), 0, 0)),
                  _const_spec((1, D)), _const_spec((D, 2 * D)), _const_spec((1, 2 * D))],
        out_specs=tok,
        compiler_params=_cparams(("arbitrary", "arbitrary")),
        name="conv_in",
    )(x, mods, w["norm_mix"], w["conv_w_pw1"], w["conv_b_pw1"])


def _conv_out_kernel(*refs, rows, nb, tl, has_halo):
    if has_halo:
        (x_ref, mod_ref, u_ref, up_ref, un_ref, wdw_ref, bdw_ref, lng_ref, lnb_ref, w2_ref, b2_ref,
         nrm_ref, wg_ref, wu_ref, wd_ref, y_ref, ext_ref, act_ref) = refs
    else:
        (x_ref, mod_ref, u_ref, wdw_ref, bdw_ref, lng_ref, lnb_ref, w2_ref, b2_ref,
         nrm_ref, wg_ref, wu_ref, wd_ref, y_ref, ext_ref, act_ref) = refs
    zero_halo = jnp.zeros((nb, HALO, D), jnp.float32)
    ext_ref[:, HALO:HALO + tl, :] = u_ref[...]
    if has_halo:
        li = pl.program_id(1)
        ext_ref[:, 0:HALO, :] = jnp.where(li == 0, zero_halo, up_ref[...])
        ext_ref[:, HALO + tl:, :] = jnp.where(li == pl.num_programs(1) - 1, zero_halo, un_ref[...])
    else:
        ext_ref[:, 0:HALO, :] = zero_halo
        ext_ref[:, HALO + tl:, :] = zero_halo
    win = CONV_ROWS + 2 * HALO
    for b in range(nb):
        for c0 in range(0, tl, CONV_ROWS):
            tiles = []
            for lt in range(D // LANES):
                ls = slice(lt * LANES, (lt + 1) * LANES)
                xw = ext_ref[b, c0:c0 + win, ls]
                acc = None
                for r in range(8):
                    shifted = xw if r == 0 else pltpu.roll(xw, win - r, 0)
                    for a in range(win // 8):
                        tap = 8 * a + r - (HALO - CONV_PAD)
                        if 0 <= tap < CONV_WIDTH:
                            term = shifted[8 * a:8 * a + CONV_ROWS] * wdw_ref[tap:tap + 1, ls]
                            acc = term if acc is None else acc + term
                tiles.append(acc)
            cv = jnp.concatenate(tiles, axis=1) + bdw_ref[...]
            mu = jnp.mean(cv, axis=-1, keepdims=True)
            cc = cv - mu
            var = jnp.mean(cc * cc, axis=-1, keepdims=True)
            ln = cc * lax.rsqrt(var + EPS) * lng_ref[...] + lnb_ref[...]
            act_ref[b * tl + c0:b * tl + c0 + CONV_ROWS, :] = _bf(_silu(ln))
    out = _dot(act_ref[...], w2_ref[...]) + b2_ref[...]
    x = x_ref[...].reshape(rows, D)
    m = mod_ref[0]
    x1 = x + m[2:3] * out
    h2 = _modulate(x1, nrm_ref[...], m[3:4], m[4:5])
    y = x1 + m[5:6] * _ffn(_bf(h2), wg_ref, wu_ref, wd_ref)
    y_ref[...] = y.reshape(y_ref.shape)


def _conv_out_call(x, mods, mod_row, u, w, nb, tl):
    bsz, seq, _ = x.shape
    assert bsz % nb == 0 and seq % tl == 0
    has_halo = tl < seq
    assert has_halo or tl == seq
    assert not (has_halo and nb != 1)
    tok = pl.BlockSpec((nb, tl, D), lambda b, l: (b, l, 0))
    in_specs = [tok, pl.BlockSpec((1, 6, D), lambda b, l: (mod_row(b), 0, 0)), tok]
    args = [x, mods, u]
    if has_halo:
        per = tl // HALO
        nhb = seq // HALO
        in_specs += [pl.BlockSpec((1, HALO, D), lambda b, l: (b, jnp.maximum(l * per - 1, 0), 0)),
                     pl.BlockSpec((1, HALO, D), lambda b, l: (b, jnp.minimum((l + 1) * per, nhb - 1), 0))]
        args += [u, u]
    in_specs += [_const_spec((CONV_WIDTH, D)), _const_spec((1, D)), _const_spec((1, D)), _const_spec((1, D)),
                 _const_spec((D, D)), _const_spec((1, D)), _const_spec((1, D))] + _ffn_specs()
    args += [w["conv_w_dw"], w["conv_b_dw"], w["conv_ln_g"], w["conv_ln_b"], w["conv_w_pw2"],
             w["conv_b_pw2"], w["norm_ffn"], w["ffn_gate"], w["ffn_up"], w["ffn_down"]]
    return pl.pallas_call(
        functools.partial(_conv_out_kernel, rows=nb * tl, nb=nb, tl=tl, has_halo=has_halo),
        out_shape=jax.ShapeDtypeStruct(x.shape, jnp.float32),
        grid=(bsz // nb, seq // tl),
        in_specs=in_specs,
        out_specs=tok,
        scratch_shapes=[pltpu.VMEM((nb, tl + 2 * HALO, D), jnp.float32),
                        pltpu.VMEM((nb * tl, D), jnp.bfloat16)],
        compiler_params=_cparams(("arbitrary", "arbitrary")),
        name="conv_out_ffn",
    )(*args)


def _prep_even(j, w_in, qa_norm, w_uq, kva_norm, w_ukv, q_norm, k_norm, w_alpha_up, b_alpha, gla_norm,
               w_out):
    wi = w_in[j]
    o = np.cumsum((0, Q_LORA, KV_LORA, QK_ROPE, GQ_W, GQ_W, GV_W, GV_W, 2 * ALPHA_RANK))
    cq, ckv, kr, gq, gk, gv, gg, ga = (wi[:, o[i]:o[i + 1]] for i in range(8))
    zeros = lambda n: jnp.zeros((D, n), wi.dtype)
    xs = jnp.concatenate([kr, ga, zeros(SLAB - QK_ROPE - 2 * ALPHA_RANK)], axis=1)
    krs = jnp.concatenate([zeros(QK_NOPE), kr, zeros(SLAB - QK_HEAD)], axis=1)
    w_in_p = jnp.concatenate([cq, ckv, xs, krs, gq * (GLA_DK ** -0.5), gk, gv, gg], axis=1)

    pad_head = lambda t: jnp.pad(t, [(0, 0)] * (t.ndim - 1) + [(0, SLAB - t.shape[-1])])
    w_uq_p = pad_head(w_uq[j].reshape(Q_LORA, MLA_HEADS, QK_HEAD)).reshape(Q_LORA, QK_W)
    ukv = w_ukv[j].reshape(KV_LORA, MLA_HEADS, QK_NOPE + V_HEAD)
    w_k = pad_head(ukv[:, :, :QK_NOPE]).reshape(KV_LORA, QK_W)
    w_v = ukv[:, :, QK_NOPE:].reshape(KV_LORA, V_W)
    gain = lambda g: jnp.tile(pad_head(g), MLA_HEADS).reshape(1, QK_W)

    w_al = jnp.zeros((SLAB, 2 * GQ_W), jnp.float32)
    for d in range(2):
        r0 = QK_ROPE + d * ALPHA_RANK
        w_al = w_al.at[r0:r0 + ALPHA_RANK, d * GQ_W:(d + 1) * GQ_W].set(w_alpha_up[j, d])
    return dict(
        w_in=_bf(w_in_p), qa_norm=qa_norm[j].reshape(1, -1), w_uq=_bf(w_uq_p),
        kva_norm=kva_norm[j].reshape(1, -1), w_ukv=_bf(jnp.concatenate([w_k, w_v], axis=1)),
        q_gain=gain(q_norm[j]) * (math.log2(math.e) / math.sqrt(QK_HEAD)), k_gain=gain(k_norm[j]),
        w_alpha=_bf(w_al), b_alpha=b_alpha[j].reshape(1, 2 * GQ_W), gla_norm=gla_norm[j].reshape(1, -1),
        w_out_attn=_bf(w_out[j, :V_W]), w_out_gla=_bf(w_out[j, V_W:]))


def _prep_ffn(i, norm_ffn, ffn_w_gate, ffn_w_up, ffn_w_down):
    return dict(norm_ffn=norm_ffn[i].reshape(1, D), ffn_gate=_bf(ffn_w_gate[i]), ffn_up=_bf(ffn_w_up[i]),
                ffn_down=_bf(ffn_w_down[i]))


def _trunk(x, mods, mod_row, w0, w1, rope_t, cache, s0, gla_consts, nb, tl, tq, nba, nbg, gla_single_buffer):
    q, k, v, gq, gk, gv, gg, la, ckv_n, kr = _mix_in_call(x, mods[0], mod_row, w0, rope_t, nb, tl)
    attn = _attn_call(q, k, v, cache, tq, nba)
    o, s_fin = _gla_call(gq, gk, gv, la, s0, gla_consts, nbg, gla_single_buffer)
    x = _mix_out_call(x, mods[0], mod_row, attn, o, gg, w0, nb, tl)
    u = _conv_in_call(x, mods[1], mod_row, w1, nb, tl)
    x = _conv_out_call(x, mods[1], mod_row, u, w1, nb, tl)
    return x, ckv_n, kr, s_fin


def kernel(x_prompt, x_sample, cache_mla_ckv, cache_mla_krope, state_gla, c, c_ctx, w_ada, b_ada, norm_mix,
           norm_ffn, w_in, qa_norm, w_uq, kva_norm, w_ukv, q_norm, k_norm, w_alpha_up, b_alpha, gla_norm,
           w_out, conv_w_pw1, conv_b_pw1, conv_w_dw, conv_b_dw, conv_ln_g, conv_ln_b, conv_w_pw2,
           conv_b_pw2, ffn_w_gate, ffn_w_up, ffn_w_down):
    dec_batch = c.shape[0]
    ctx_row = dec_batch
    cv = jnp.concatenate([c, c_ctx[None], jnp.zeros((8 - dec_batch - 1, D), c.dtype)], axis=0)
    mods = _ada_call(cv, w_ada, b_ada).reshape(w_ada.shape[0], 8, 6, D)

    w0 = _prep_even(0, w_in, qa_norm, w_uq, kva_norm, w_ukv, q_norm, k_norm, w_alpha_up, b_alpha,
                    gla_norm, w_out)
    w0.update(_prep_ffn(0, norm_ffn, ffn_w_gate, ffn_w_up, ffn_w_down), norm_mix=norm_mix[0].reshape(1, D))
    w1 = dict(norm_mix=norm_mix[1].reshape(1, D), conv_w_pw1=_bf(conv_w_pw1[0]),
              conv_b_pw1=conv_b_pw1[0].reshape(1, -1), conv_w_dw=conv_w_dw[0],
              conv_b_dw=conv_b_dw[0].reshape(1, D), conv_ln_g=conv_ln_g[0].reshape(1, D),
              conv_ln_b=conv_ln_b[0].reshape(1, D), conv_w_pw2=_bf(conv_w_pw2[0]),
              conv_b_pw2=conv_b_pw2[0].reshape(1, D))
    w1.update(_prep_ffn(1, norm_ffn, ffn_w_gate, ffn_w_up, ffn_w_down))
    gla_consts = _gla_consts()

    seq = x_prompt.shape[1]
    y_prompt, ckv_n, kr, s_fin = _trunk(x_prompt, mods, lambda b: ctx_row, w0, w1, None, None, None,
                                        gla_consts, nb=2, tl=seq, tq=seq, nba=4, nbg=2, gla_single_buffer=False)

    rope_t = _rope_tables(x_sample.shape[1])
    krs_c = jnp.pad(cache_mla_krope[:, 0], ((0, 0), (0, 0), (QK_NOPE, SLAB - QK_HEAD)))
    cache = _kv_cache_call(cache_mla_ckv[:, 0], krs_c, w0)
    y_sample, _, _, _ = _trunk(x_sample, mods, lambda b: b, w0, w1, rope_t, cache, state_gla[:, 0],
                               gla_consts, nb=1, tl=512, tq=256, nba=1, nbg=1, gla_single_buffer=True)

    return (y_prompt, y_sample, ckv_n[:, None], kr[:, None], s_fin[:, None])
```

```python
import functools
import math

import numpy as np
import jax
import jax.numpy as jnp
from jax import lax
from jax.experimental import pallas as pl
from jax.experimental.pallas import tpu as pltpu

D = 1024
GRID_W = 64
MLA_HEADS = 8
Q_LORA = 384
KV_LORA = 256
QK_NOPE = 64
QK_ROPE = 32
QK_HEAD = QK_NOPE + QK_ROPE
V_HEAD = 64
ROPE_THETA = 10000.0
GLA_HEADS = 4
GLA_DK = 64
GLA_DV = 128
ALPHA_RANK = 16
GLA_TAU = 16.0
CONV_WIDTH = 31
CONV_PAD = CONV_WIDTH // 2
FFN_HIDDEN = 2816
EPS = 1e-6

LANES = 128
MXU_COLS = 256
VMEM_LIMIT = 56 * 1024 * 1024

SLAB = LANES
QK_W = MLA_HEADS * SLAB
V_W = MLA_HEADS * V_HEAD
GQ_W = GLA_HEADS * GLA_DK
GV_W = GLA_HEADS * GLA_DV
FFN_CHUNK = MXU_COLS
FFN_NCHUNK = FFN_HIDDEN // FFN_CHUNK
HALO = 16
GLA_BLK = 16
GLA_MC = 128
GLA_NBLK = GLA_MC // GLA_BLK
CONV_ROWS = 64
ATTN_KCHUNK = 2 * MXU_COLS

_OFF_CQ = 0
_OFF_CKV = _OFF_CQ + Q_LORA
_OFF_XS = _OFF_CKV + KV_LORA
_OFF_KRS = _OFF_XS + SLAB
_OFF_GQ = _OFF_KRS + SLAB
_OFF_GK = _OFF_GQ + GQ_W
_OFF_GV = _OFF_GK + GQ_W
_OFF_GG = _OFF_GV + GV_W
IN_W_PAD = _OFF_GG + GV_W


def _bf(x):
    return x.astype(jnp.bfloat16)


def _dot(a, b):
    return jnp.dot(a, b, preferred_element_type=jnp.float32)


def _dot_nt(a, b):
    return lax.dot_general(a, b, (((1,), (1,)), ((), ())), preferred_element_type=jnp.float32)


def _sigmoid(x):
    return 1.0 / (1.0 + jnp.exp(-x))


def _silu(x):
    return x * _sigmoid(x)


def _rms(x, n=None):
    n = x.shape[-1] if n is None else n
    ss = jnp.sum(x * x, axis=-1, keepdims=True) * (1.0 / n)
    return x * lax.rsqrt(ss + EPS)


def _modulate(x, g, shift, scale):
    return _rms(x) * g * (1.0 + scale) + shift


def _cparams(sem):
    return pltpu.CompilerParams(dimension_semantics=sem, vmem_limit_bytes=VMEM_LIMIT)


def _const_spec(shape):
    nd = len(shape)
    return pl.BlockSpec(shape, lambda *_: (0,) * nd, pipeline_mode=pl.Buffered(1))


def _ada_kernel(c_ref, w_ref, b_ref, o_ref):
    s = _bf(_silu(c_ref[...]))
    o_ref[0] = _dot(s, _bf(w_ref[0])) + b_ref[0]


def _ada_call(cv, w_ada, b_ada):
    depth = w_ada.shape[0]
    tn = 1536
    return pl.pallas_call(
        _ada_kernel,
        out_shape=jax.ShapeDtypeStruct((depth, 8, 6 * D), jnp.float32),
        grid=(depth, 6 * D // tn),
        in_specs=[pl.BlockSpec((8, D), lambda i, j: (0, 0)),
                  pl.BlockSpec((1, D, tn), lambda i, j: (i, 0, j)),
                  pl.BlockSpec((1, 1, tn), lambda i, j: (i, 0, j))],
        out_specs=pl.BlockSpec((1, 8, tn), lambda i, j: (i, 0, j)),
        compiler_params=_cparams(("arbitrary", "arbitrary")),
        name="ada_params",
    )(cv, w_ada, b_ada.reshape(depth, 1, 6 * D))


def _rope_kernel(f_ref, isrow_ref, sa_ref, sb_ref, c_ref, s_ref):
    n = c_ref.shape[0]
    t = lax.broadcasted_iota(jnp.int32, (n, SLAB), 0)
    r = (t // GRID_W).astype(jnp.float32)
    col = (t % GRID_W).astype(jnp.float32)
    pos = jnp.where(isrow_ref[...] > 0.5, r, col)
    ang = pos * f_ref[...]
    cs = jnp.cos(ang)
    sn = jnp.sin(ang)
    rope_lane = (sa_ref[...] != 0.0) | (sb_ref[...] != 0.0)
    c_ref[...] = jnp.where(rope_lane, cs, 1.0)
    s_ref[...] = sn * (sa_ref[...] + sb_ref[...])


def _rope_tables(n_tokens):
    half = QK_ROPE // 2
    m = half // 2
    inv_freq = ROPE_THETA ** (-np.arange(0, half, 2, dtype=np.float32) / half)
    f = np.zeros((1, SLAB), np.float32)
    isrow = np.zeros((1, SLAB), np.float32)
    sa = np.zeros((1, SLAB), np.float32)
    sb = np.zeros((1, SLAB), np.float32)
    for axis in range(2):
        base = QK_NOPE + axis * half
        f[0, base:base + m] = inv_freq
        f[0, base + m:base + half] = inv_freq
        isrow[0, base:base + half] = 1.0 if axis == 0 else 0.0
        sa[0, base:base + m] = -1.0
        sb[0, base + m:base + half] = 1.0
    out = jax.ShapeDtypeStruct((n_tokens, SLAB), jnp.float32)
    partner = np.zeros((SLAB, SLAB), np.float32)
    for dst in range(SLAB):
        if sa[0, dst] != 0.0:
            partner[dst + m, dst] = 1.0
        if sb[0, dst] != 0.0:
            partner[dst - m, dst] = 1.0
    c_tab, s_tab = pl.pallas_call(_rope_kernel, out_shape=(out, out), name="rope_tables")(
        jnp.asarray(f), jnp.asarray(isrow), jnp.asarray(sa), jnp.asarray(sb))
    return c_tab, s_tab, jnp.asarray(partner, jnp.bfloat16)


def _head_norm_rope(slab, gain, rope):
    y = _rms(slab, QK_HEAD) * gain
    if rope is not None:
        c, s, partner = rope
        y = y * c + _dot(_bf(y), partner) * s
    return y


def _store(ref, val, lanes=slice(None)):
    if len(ref.shape) == 3:
        ref[:, :, lanes] = val.reshape(ref.shape[0], ref.shape[1], val.shape[-1])
    else:
        ref[:, lanes] = val


def _keys_values(ckv_n, krs, wukv_ref, kg_ref, rope, k_ref, v_ref):
    kv = _dot(_bf(ckv_n), wukv_ref[...])
    for h in range(MLA_HEADS):
        sl = slice(h * SLAB, (h + 1) * SLAB)
        _store(k_ref, _bf(_head_norm_rope(kv[:, sl] + krs, kg_ref[:, sl], rope)), sl)
    _store(v_ref, _bf(kv[:, QK_W:]))


def _mix_in_kernel(*refs, rope, rows):
    (x_ref, mod_ref, nrm_ref, win_ref, qan_ref, wuq_ref, kvan_ref, wukv_ref, qg_ref, kg_ref,
     wal_ref, bal_ref) = refs[:12]
    pos = 12
    rope_t = None
    if rope:
        rope_t = (refs[12][...], refs[13][...], refs[14][...])
        pos = 15
    q_ref, k_ref, v_ref, gq_ref, gk_ref, gv_ref, gg_ref, la_ref, ckv_ref, kr_ref = refs[pos:]

    x = x_ref[...].reshape(rows, D)
    m = mod_ref[0]
    h = _modulate(x, nrm_ref[...], m[0:1], m[1:2])
    proj = _dot(_bf(h), win_ref[...])
    cq = proj[:, _OFF_CQ:_OFF_CQ + Q_LORA]
    ckv = proj[:, _OFF_CKV:_OFF_CKV + KV_LORA]
    xs = proj[:, _OFF_XS:_OFF_XS + SLAB]
    krs = proj[:, _OFF_KRS:_OFF_KRS + SLAB]

    q = _dot(_bf(_rms(cq) * qan_ref[...]), wuq_ref[...])
    for hd in range(MLA_HEADS):
        sl = slice(hd * SLAB, (hd + 1) * SLAB)
        _store(q_ref, _bf(_head_norm_rope(q[:, sl], qg_ref[:, sl], rope_t)), sl)

    ckv_n = _rms(ckv) * kvan_ref[...]
    _store(ckv_ref, ckv_n)
    _store(kr_ref, xs[:, :QK_ROPE])
    _keys_values(ckv_n, krs, wukv_ref, kg_ref, rope_t, k_ref, v_ref)

    _store(gq_ref, _bf(proj[:, _OFF_GQ:_OFF_GQ + GQ_W]))
    _store(gk_ref, _bf(proj[:, _OFF_GK:_OFF_GK + GQ_W]))
    _store(gv_ref, _bf(proj[:, _OFF_GV:_OFF_GV + GV_W]))
    _store(gg_ref, proj[:, _OFF_GG:_OFF_GG + GV_W])
    z = _dot(_bf(xs), wal_ref[...]) + bal_ref[...]
    log_sig = jnp.minimum(z, 0.0) - jnp.log1p(jnp.exp(-jnp.abs(z)))
    _store(la_ref, log_sig * (1.0 / GLA_TAU))


def _mix_in_call(x, mods, mod_row, w, rope_t, nb, tl):
    bsz, seq, _ = x.shape
    assert bsz % nb == 0 and seq % tl == 0
    rows = nb * tl
    grid = (bsz // nb, seq // tl)
    tok = lambda width: pl.BlockSpec((nb, tl, width), lambda b, l: (b, l, 0))
    in_specs = [tok(D),
                pl.BlockSpec((1, 6, D), lambda b, l: (mod_row(b), 0, 0)),
                _const_spec((1, D)), _const_spec((D, IN_W_PAD)), _const_spec((1, Q_LORA)),
                _const_spec((Q_LORA, QK_W)), _const_spec((1, KV_LORA)),
                _const_spec((KV_LORA, QK_W + V_W)), _const_spec((1, QK_W)), _const_spec((1, QK_W)),
                _const_spec((SLAB, 2 * GQ_W)), _const_spec((1, 2 * GQ_W))]
    args = [x, mods, w["norm_mix"], w["w_in"], w["qa_norm"], w["w_uq"], w["kva_norm"], w["w_ukv"],
            w["q_gain"], w["k_gain"], w["w_alpha"], w["b_alpha"]]
    if rope_t is not None:
        assert nb == 1
        in_specs += [pl.BlockSpec((tl, SLAB), lambda b, l: (l, 0))] * 2 + [_const_spec((SLAB, SLAB))]
        args += list(rope_t)
    widths = [(QK_W, jnp.bfloat16), (QK_W, jnp.bfloat16), (V_W, jnp.bfloat16), (GQ_W, jnp.bfloat16),
              (GQ_W, jnp.bfloat16), (GV_W, jnp.bfloat16), (GV_W, jnp.float32), (2 * GQ_W, jnp.float32),
              (KV_LORA, jnp.float32), (QK_ROPE, jnp.float32)]
    return pl.pallas_call(
        functools.partial(_mix_in_kernel, rope=rope_t is not None, rows=rows),
        out_shape=tuple(jax.ShapeDtypeStruct((bsz, seq, wd), dt) for wd, dt in widths),
        grid=grid,
        in_specs=in_specs,
        out_specs=tuple(tok(wd) for wd, _ in widths),
        compiler_params=_cparams(("arbitrary", "arbitrary")),
        name="mix_in",
    )(*args)


def _kv_cache_kernel(ckv_ref, krs_ref, wukv_ref, kg_ref, k_ref, v_ref):
    _keys_values(ckv_ref[0], krs_ref[0], wukv_ref, kg_ref, None, k_ref, v_ref)


def _kv_cache_call(ckv_c, krs_c, w):
    bsz, past, _ = ckv_c.shape
    blk = lambda width: pl.BlockSpec((1, past, width), lambda b: (b, 0, 0))
    return pl.pallas_call(
        _kv_cache_kernel,
        out_shape=(jax.ShapeDtypeStruct((bsz, past, QK_W), jnp.bfloat16),
                   jax.ShapeDtypeStruct((bsz, past, V_W), jnp.bfloat16)),
        grid=(bsz,),
        in_specs=[blk(KV_LORA), blk(SLAB), _const_spec((KV_LORA, QK_W + V_W)), _const_spec((1, QK_W))],
        out_specs=(blk(QK_W), blk(V_W)),
        compiler_params=_cparams(("arbitrary",)),
        name="kv_cache",
    )(ckv_c, krs_c, w["w_ukv"], w["k_gain"])


def _attn_kernel(*refs, has_cache, nba):
    if has_cache:
        q_ref, kn_ref, vn_ref, kc_ref, vc_ref, o_ref, va_ref = refs
    else:
        q_ref, kn_ref, vn_ref, o_ref, va_ref = refs
        kc_ref = vc_ref = None
    lk = kn_ref.shape[1]
    lane = lax.broadcasted_iota(jnp.int32, (1, 2 * V_HEAD), 1)
    lo = lane < V_HEAD

    @pl.when(pl.program_id(2) == 0)
    def _():
        for b in range(nba):
            for h in range(2):
                keep = lo if h == 0 else jnp.logical_not(lo)
                vn = vn_ref[b]
                va_ref[b, h, 0:lk, :] = jnp.where(keep, vn, jnp.ones_like(vn))
                if has_cache:
                    vc = vc_ref[b]
                    va_ref[b, h, lk:, :] = jnp.where(keep, vc, jnp.ones_like(vc))

    chunks = [(kn_ref, c0, c0, min(ATTN_KCHUNK, lk - c0)) for c0 in range(0, lk, ATTN_KCHUNK)]
    if has_cache:
        chunks.append((kc_ref, 0, lk, kc_ref.shape[1]))
    for b in range(nba):
        scores = []
        for h in range(2):
            sl = slice(h * SLAB, (h + 1) * SLAB)
            qh = q_ref[b, :, sl]
            parts = []
            m_el = None
            for k_ref, r0, _, n in chunks:
                s = _dot_nt(qh, k_ref[b, r0:r0 + n, sl])
                parts.append(s)
                for c in range(0, n, LANES):
                    m_el = s[:, c:c + LANES] if m_el is None else jnp.maximum(m_el, s[:, c:c + LANES])
            scores.append((parts, jnp.max(m_el, axis=-1, keepdims=True)))
        outs = []
        for h in range(2):
            parts, mx = scores[h]
            o = None
            for s, (_, _, v0, n) in zip(parts, chunks):
                term = _dot(_bf(jnp.exp2(s - mx)), va_ref[b, h, v0:v0 + n, :])
                o = term if o is None else o + term
            outs.append(o / pltpu.roll(o, V_HEAD, 1))
        o_ref[b] = _bf(jnp.where(lo, outs[0], outs[1]))


def _attn_call(q, kn, vn, cache, tq, nba):
    bsz, seq, _ = q.shape
    assert bsz % nba == 0 and seq % tq == 0
    lk = kn.shape[1]
    npair = MLA_HEADS // 2
    in_specs = [pl.BlockSpec((nba, tq, 2 * SLAB), lambda b, p, i: (b, i, p)),
                pl.BlockSpec((nba, lk, 2 * SLAB), lambda b, p, i: (b, 0, p)),
                pl.BlockSpec((nba, lk, 2 * V_HEAD), lambda b, p, i: (b, 0, p))]
    args = [q, kn, vn]
    past = 0
    if cache is not None:
        past = cache[0].shape[1]
        in_specs += [pl.BlockSpec((nba, past, 2 * SLAB), lambda b, p, i: (b, 0, p)),
                     pl.BlockSpec((nba, past, 2 * V_HEAD), lambda b, p, i: (b, 0, p))]
        args += list(cache)
    return pl.pallas_call(
        functools.partial(_attn_kernel, has_cache=cache is not None, nba=nba),
        out_shape=jax.ShapeDtypeStruct((bsz, seq, V_W), jnp.bfloat16),
        grid=(bsz // nba, npair, seq // tq),
        in_specs=in_specs,
        out_specs=pl.BlockSpec((nba, tq, 2 * V_HEAD), lambda b, p, i: (b, i, p)),
        scratch_shapes=[pltpu.VMEM((nba, 2, lk + past, 2 * V_HEAD), jnp.bfloat16)],
        compiler_params=_cparams(("arbitrary", "arbitrary", "arbitrary")),
        name="mla_attention",
    )(*args)


def _gla_consts():
    t = np.arange(GLA_MC)
    same = (t[:, None] // GLA_BLK) == (t[None, :] // GLA_BLK)
    tri_f = same & (t[None, :] <= t[:, None])
    tri_b = same & (t[None, :] >= t[:, None])
    sums = np.stack([np.concatenate([tri_f, same], 0), np.concatenate([tri_b, same], 0)]).astype(np.float32)
    red = np.zeros((GLA_BLK, 2, GLA_DK, 2, GLA_NBLK, GLA_BLK), np.float32)
    for jj in range(GLA_BLK):
        for hd in range(2):
            red[jj, hd, :, hd, :, jj] = 1.0
    red = red.reshape(GLA_BLK * 2 * GLA_DK, 2 * GLA_MC)
    return jnp.asarray(sums, jnp.bfloat16), jnp.asarray(red, jnp.bfloat16)


def _split3(g):
    g1 = _bf(g)
    r1 = g - g1.astype(jnp.float32)
    g2 = _bf(r1)
    g3 = _bf(r1 - g2.astype(jnp.float32))
    return jnp.concatenate([g1, g2, g3], axis=1)


def _gla_mask(backward):
    row = lax.broadcasted_iota(jnp.int32, (GLA_MC, 2 * GLA_MC), 0)
    colv = lax.broadcasted_iota(jnp.int32, (GLA_MC, 2 * GLA_MC), 1)
    jj_of = colv % GLA_BLK
    in_blk = (colv // GLA_BLK) % GLA_NBLK == row // GLA_BLK
    causal = (jj_of >= row % GLA_BLK) if backward else (jj_of <= row % GLA_BLK)
    return in_blk & causal


def _gla_chunks(chains, sums_ref, red_ref, rows_ref, masks):
    n = len(chains)
    two_dk = 2 * GLA_DK
    dirs = [ch[5] for ch in chains]
    vb = [_bf(ch[3]) for ch in chains]

    parts = [_dot(sums_ref[ch[5]], _split3(ch[2])) for ch in chains]
    parts = [p[:, :two_dk] + p[:, two_dk:2 * two_dk] + p[:, 2 * two_dk:] for p in parts]
    cum = [p[:GLA_MC] for p in parts]
    tot = [p[GLA_MC:] for p in parts]

    qh = [_bf(chains[i][0] * jnp.exp(cum[i])) for i in range(n)]
    kh_t = [_bf(jnp.transpose(chains[i][1] * jnp.exp(tot[i] - cum[i]))) for i in range(n)]
    dec_t = [jnp.transpose(jnp.exp(tot[i])) for i in range(n)]
    blk_lane = lax.broadcasted_iota(jnp.int32, (1, GLA_MC), 1) // GLA_BLK

    upds = [[None] * GLA_NBLK for _ in range(n)]
    ws = [[] for _ in range(n)]
    for i in range(n):
        rows_ref[i, 0] = chains[i][1]
        rows_ref[i, 1] = cum[i]

    def block_rows(i, which, jj):
        return jnp.concatenate(
            [jnp.broadcast_to(rows_ref[i, which, blk * GLA_BLK + jj:blk * GLA_BLK + jj + 1, :], (GLA_BLK, two_dk))
             for blk in range(GLA_NBLK)], axis=0)

    for jj in range(GLA_BLK):
        for i in range(n):
            e = jnp.exp(jnp.minimum(cum[i] - block_rows(i, 1, jj), 0.0))
            ws[i].append(_bf(chains[i][0] * block_rows(i, 0, jj) * e))
        if jj % 2 == 1:
            blk = jj // 2
            for i in range(n):
                k_sel = jnp.where(blk_lane == blk, kh_t[i], jnp.zeros_like(kh_t[i]))
                upds[i][blk] = [_dot(k_sel[hd * GLA_DK:(hd + 1) * GLA_DK],
                                     vb[i][:, hd * GLA_DV:(hd + 1) * GLA_DV]) for hd in range(2)]
    zeros_v = jnp.zeros((GLA_MC, GLA_DV), jnp.bfloat16)
    o_intra = []
    for i in range(n):
        a = _dot(jnp.concatenate(ws[i], axis=1), red_ref[...])
        a = _bf(jnp.where(masks[dirs[i]], a, 0.0))
        v_bd = jnp.concatenate([jnp.concatenate([vb[i][:, :GLA_DV], zeros_v], axis=1),
                                jnp.concatenate([zeros_v, vb[i][:, GLA_DV:]], axis=1)], axis=0)
        o_intra.append(_dot(a, v_bd))

    o_blocks = [[None] * GLA_NBLK for _ in range(n)]
    zeros_s = jnp.zeros((GLA_DK, GLA_DV), jnp.bfloat16)
    st = [list(ch[4]) for ch in chains]
    for step in range(GLA_NBLK):
        for i in range(n):
            blk = GLA_NBLK - 1 - step if dirs[i] else step
            rs = slice(blk * GLA_BLK, (blk + 1) * GLA_BLK)
            s_bd = jnp.concatenate([jnp.concatenate([_bf(st[i][0]), zeros_s], axis=1),
                                    jnp.concatenate([zeros_s, _bf(st[i][1])], axis=1)], axis=0)
            o_blocks[i][blk] = o_intra[i][rs] + _dot(qh[i][rs], s_bd)
            col = blk * GLA_BLK
            for hd in range(2):
                dcol = dec_t[i][hd * GLA_DK:(hd + 1) * GLA_DK, col:col + 1]
                st[i][hd] = st[i][hd] * dcol + upds[i][blk][hd]
    return [jnp.concatenate(ob, axis=0) for ob in o_blocks], [tuple(s) for s in st]


def _gla_kernel(*refs, has_state, nbg):
    if has_state:
        q_ref, k_ref, v_ref, la_ref, sums_ref, red_ref, s0_ref, o_ref, sf_ref, rows_ref = refs
    else:
        q_ref, k_ref, v_ref, la_ref, sums_ref, red_ref, o_ref, sf_ref, rows_ref = refs
    n_mc = q_ref.shape[1] // GLA_MC
    npair = GLA_HEADS // 2
    two_dk = 2 * GLA_DK
    two_dv = 2 * GLA_DV
    z = jnp.zeros((GLA_DK, GLA_DV), jnp.float32)
    chains = [(bi, hp, d) for bi in range(nbg) for hp in range(npair) for d in range(2)]
    st0 = tuple(tuple(s0_ref[bi, d, 2 * hp + hd] if has_state else z for hd in range(2))
                for bi, hp, d in chains)
    masks = (_gla_mask(False), _gla_mask(True))
    o_ref[...] = jnp.zeros(o_ref.shape, o_ref.dtype)

    def body(c, carry):
        work, where = [], []
        for (bi, hp, d), st in zip(chains, carry):
            r0 = pl.multiple_of(((n_mc - 1 - c) if d else c) * GLA_MC, GLA_MC)
            rows = pl.ds(r0, GLA_MC)
            qk_l = slice(hp * two_dk, (hp + 1) * two_dk)
            v_l = slice(hp * two_dv, (hp + 1) * two_dv)
            la_l = slice(d * GQ_W + hp * two_dk, d * GQ_W + (hp + 1) * two_dk)
            work.append((q_ref[bi, rows, qk_l].astype(jnp.float32), k_ref[bi, rows, qk_l].astype(jnp.float32),
                         la_ref[bi, rows, la_l], v_ref[bi, rows, v_l], st, d))
            where.append((bi, rows, v_l))
        outs, new = _gla_chunks(work, sums_ref, red_ref, rows_ref, masks)
        for (bi, rows, v_l), o in zip(where, outs):
            o_ref[bi, rows, v_l] += o
        return tuple(new)

    finals = lax.fori_loop(0, n_mc, body, st0)
    for (bi, hp, d), st in zip(chains, finals):
        for hd in range(2):
            sf_ref[bi, d, 2 * hp + hd] = st[hd]


def _gla_call(gq, gk, gv, la, s0, consts, nbg, single_buffer):
    bsz, seq, _ = gq.shape
    assert bsz % nbg == 0 and seq % GLA_MC == 0
    sums, red = consts
    mode = dict(pipeline_mode=pl.Buffered(1)) if single_buffer else {}
    tok = lambda width: pl.BlockSpec((nbg, seq, width), lambda b: (b, 0, 0), **mode)
    st_spec = pl.BlockSpec((nbg, 2, GLA_HEADS, GLA_DK, GLA_DV), lambda b: (b, 0, 0, 0, 0))
    in_specs = [tok(GQ_W), tok(GQ_W), tok(GV_W), tok(2 * GQ_W), _const_spec(sums.shape), _const_spec(red.shape)]
    args = [gq, gk, gv, la, sums, red]
    if s0 is not None:
        in_specs.append(st_spec)
        args.append(s0)
    return pl.pallas_call(
        functools.partial(_gla_kernel, has_state=s0 is not None, nbg=nbg),
        out_shape=(jax.ShapeDtypeStruct((bsz, seq, GV_W), jnp.float32),
                   jax.ShapeDtypeStruct((bsz, 2, GLA_HEADS, GLA_DK, GLA_DV), jnp.float32)),
        grid=(bsz // nbg,),
        in_specs=in_specs,
        out_specs=(pl.BlockSpec((nbg, seq, GV_W), lambda b: (b, 0, 0)), st_spec),
        scratch_shapes=[pltpu.VMEM((nbg * GLA_HEADS, 2, GLA_MC, 2 * GLA_DK), jnp.float32)],
        compiler_params=_cparams(("arbitrary",)),
        name="gla_bidir",
    )(*args)


def _ffn(hb, wg_ref, wu_ref, wd_ref):
    acc = None
    for c in range(FFN_NCHUNK):
        hs = slice(c * FFN_CHUNK, (c + 1) * FFN_CHUNK)
        g = _dot(hb, wg_ref[:, hs])
        u = _dot(hb, wu_ref[:, hs])
        part = _dot(_bf(_silu(g) * u), wd_ref[hs, :])
        acc = part if acc is None else acc + part
    return acc


def _ffn_specs():
    return [_const_spec((D, FFN_HIDDEN)), _const_spec((D, FFN_HIDDEN)), _const_spec((FFN_HIDDEN, D))]


def _mix_out_kernel(x_ref, mod_ref, attn_ref, o_ref, gg_ref, gn_ref, woa_ref, wog_ref, nrm_ref,
                    wg_ref, wu_ref, wd_ref, y_ref, *, rows):
    x = x_ref[...].reshape(rows, D)
    m = mod_ref[0]
    o = o_ref[...].reshape(rows, GV_W)
    gg = gg_ref[...].reshape(rows, GV_W)
    out = _dot(attn_ref[...].reshape(rows, V_W), woa_ref[...])
    gated = []
    for h in range(GLA_HEADS):
        sl = slice(h * GLA_DV, (h + 1) * GLA_DV)
        gated.append(_bf(_rms(o[:, sl]) * gn_ref[...] * _silu(gg[:, sl])))
    out = out + _dot(jnp.concatenate(gated, axis=1), wog_ref[...])
    x1 = x + m[2:3] * out
    h2 = _modulate(x1, nrm_ref[...], m[3:4], m[4:5])
    y = x1 + m[5:6] * _ffn(_bf(h2), wg_ref, wu_ref, wd_ref)
    y_ref[...] = y.reshape(y_ref.shape)


def _mix_out_call(x, mods, mod_row, attn, o, gg, w, nb, tl):
    bsz, seq, _ = x.shape
    assert bsz % nb == 0 and seq % tl == 0
    rows = nb * tl
    tok = lambda width: pl.BlockSpec((nb, tl, width), lambda b, l: (b, l, 0))
    return pl.pallas_call(
        functools.partial(_mix_out_kernel, rows=rows),
        out_shape=jax.ShapeDtypeStruct(x.shape, jnp.float32),
        grid=(bsz // nb, seq // tl),
        in_specs=[tok(D), pl.BlockSpec((1, 6, D), lambda b, l: (mod_row(b), 0, 0)),
                  tok(V_W), tok(GV_W), tok(GV_W), _const_spec((1, GLA_DV)),
                  _const_spec((V_W, D)), _const_spec((GV_W, D)), _const_spec((1, D))] + _ffn_specs(),
        out_specs=tok(D),
        compiler_params=_cparams(("arbitrary", "arbitrary")),
        name="mix_out_ffn",
    )(x, mods, attn, o, gg, w["gla_norm"], w["w_out_attn"], w["w_out_gla"], w["norm_ffn"],
      w["ffn_gate"], w["ffn_up"], w["ffn_down"])


def _conv_in_kernel(x_ref, mod_ref, nrm_ref, w1_ref, b1_ref, u_ref, *, rows):
    x = x_ref[...].reshape(rows, D)
    m = mod_ref[0]
    hb = _bf(_modulate(x, nrm_ref[...], m[0:1], m[1:2]))
    a = _dot(hb, w1_ref[:, :D]) + b1_ref[:, :D]
    g = _dot(hb, w1_ref[:, D:]) + b1_ref[:, D:]
    u_ref[...] = (a * _sigmoid(g)).reshape(u_ref.shape)


def _conv_in_call(x, mods, mod_row, w, nb, tl):
    bsz, seq, _ = x.shape
    assert bsz % nb == 0 and seq % tl == 0
    tok = pl.BlockSpec((nb, tl, D), lambda b, l: (b, l, 0))
    return pl.pallas_call(
        functools.partial(_conv_in_kernel, rows=nb * tl),
        out_shape=jax.ShapeDtypeStruct(x.shape, jnp.float32),
        grid=(bsz // nb, seq // tl),
        in_specs=[tok, pl.BlockSpec((1, 6, D), lambda b, l: (mod_row(b), 0, 0)),
                  _const_spec((1, D)), _const_spec((D, 2 * D)), _const_spec((1, 2 * D))],
        out_specs=tok,
        compiler_params=_cparams(("arbitrary", "arbitrary")),
        name="conv_in",
    )(x, mods, w["norm_mix"], w["conv_w_pw1"], w["conv_b_pw1"])


def _conv_out_kernel(*refs, rows, nb, tl, has_halo):
    if has_halo:
        (x_ref, mod_ref, u_ref, up_ref, un_ref, wdw_ref, bdw_ref, lng_ref, lnb_ref, w2_ref, b2_ref,
         nrm_ref, wg_ref, wu_ref, wd_ref, y_ref, ext_ref, act_ref) = refs
    else:
        (x_ref, mod_ref, u_ref, wdw_ref, bdw_ref, lng_ref, lnb_ref, w2_ref, b2_ref,
         nrm_ref, wg_ref, wu_ref, wd_ref, y_ref, ext_ref, act_ref) = refs
    zero_halo = jnp.zeros((nb, HALO, D), jnp.float32)
    ext_ref[:, HALO:HALO + tl, :] = u_ref[...]
    if has_halo:
        li = pl.program_id(1)
        ext_ref[:, 0:HALO, :] = jnp.where(li == 0, zero_halo, up_ref[...])
        ext_ref[:, HALO + tl:, :] = jnp.where(li == pl.num_programs(1) - 1, zero_halo, un_ref[...])
    else:
        ext_ref[:, 0:HALO, :] = zero_halo
        ext_ref[:, HALO + tl:, :] = zero_halo
    win = CONV_ROWS + 2 * HALO
    for b in range(nb):
        for c0 in range(0, tl, CONV_ROWS):
            tiles = []
            for lt in range(D // LANES):
                ls = slice(lt * LANES, (lt + 1) * LANES)
                xw = ext_ref[b, c0:c0 + win, ls]
                acc = None
                for r in range(8):
                    shifted = xw if r == 0 else pltpu.roll(xw, win - r, 0)
                    for a in range(win // 8):
                        tap = 8 * a + r - (HALO - CONV_PAD)
                        if 0 <= tap < CONV_WIDTH:
                            term = shifted[8 * a:8 * a + CONV_ROWS] * wdw_ref[tap:tap + 1, ls]
                            acc = term if acc is None else acc + term
                tiles.append(acc)
            cv = jnp.concatenate(tiles, axis=1) + bdw_ref[...]
            mu = jnp.mean(cv, axis=-1, keepdims=True)
            cc = cv - mu
            var = jnp.mean(cc * cc, axis=-1, keepdims=True)
            ln = cc * lax.rsqrt(var + EPS) * lng_ref[...] + lnb_ref[...]
            act_ref[b * tl + c0:b * tl + c0 + CONV_ROWS, :] = _bf(_silu(ln))
    out = _dot(act_ref[...], w2_ref[...]) + b2_ref[...]
    x = x_ref[...].reshape(rows, D)
    m = mod_ref[0]
    x1 = x + m[2:3] * out
    h2 = _modulate(x1, nrm_ref[...], m[3:4], m[4:5])
    y = x1 + m[5:6] * _ffn(_bf(h2), wg_ref, wu_ref, wd_ref)
    y_ref[...] = y.reshape(y_ref.shape)


def _conv_out_call(x, mods, mod_row, u, w, nb, tl):
    bsz, seq, _ = x.shape
    assert bsz % nb == 0 and seq % tl == 0
    has_halo = tl < seq
    assert has_halo or tl == seq
    assert not (has_halo and nb != 1)
    tok = pl.BlockSpec((nb, tl, D), lambda b, l: (b, l, 0))
    in_specs = [tok, pl.BlockSpec((1, 6, D), lambda b, l: (mod_row(b), 0, 0)), tok]
    args = [x, mods, u]
    if has_halo:
        per = tl // HALO
        nhb = seq // HALO
        in_specs += [pl.BlockSpec((1, HALO, D), lambda b, l: (b, jnp.maximum(l * per - 1, 0), 0)),
                     pl.BlockSpec((1, HALO, D), lambda b, l: (b, jnp.minimum((l + 1) * per, nhb - 1), 0))]
        args += [u, u]
    in_specs += [_const_spec((CONV_WIDTH, D)), _const_spec((1, D)), _const_spec((1, D)), _const_spec((1, D)),
                 _const_spec((D, D)), _const_spec((1, D)), _const_spec((1, D))] + _ffn_specs()
    args += [w["conv_w_dw"], w["conv_b_dw"], w["conv_ln_g"], w["conv_ln_b"], w["conv_w_pw2"],
             w["conv_b_pw2"], w["norm_ffn"], w["ffn_gate"], w["ffn_up"], w["ffn_down"]]
    return pl.pallas_call(
        functools.partial(_conv_out_kernel, rows=nb * tl, nb=nb, tl=tl, has_halo=has_halo),
        out_shape=jax.ShapeDtypeStruct(x.shape, jnp.float32),
        grid=(bsz // nb, seq // tl),
        in_specs=in_specs,
        out_specs=tok,
        scratch_shapes=[pltpu.VMEM((nb, tl + 2 * HALO, D), jnp.float32),
                        pltpu.VMEM((nb * tl, D), jnp.bfloat16)],
        compiler_params=_cparams(("arbitrary", "arbitrary")),
        name="conv_out_ffn",
    )(*args)


def _prep_even(j, w_in, qa_norm, w_uq, kva_norm, w_ukv, q_norm, k_norm, w_alpha_up, b_alpha, gla_norm,
               w_out):
    wi = w_in[j]
    o = np.cumsum((0, Q_LORA, KV_LORA, QK_ROPE, GQ_W, GQ_W, GV_W, GV_W, 2 * ALPHA_RANK))
    cq, ckv, kr, gq, gk, gv, gg, ga = (wi[:, o[i]:o[i + 1]] for i in range(8))
    zeros = lambda n: jnp.zeros((D, n), wi.dtype)
    xs = jnp.concatenate([kr, ga, zeros(SLAB - QK_ROPE - 2 * ALPHA_RANK)], axis=1)
    krs = jnp.concatenate([zeros(QK_NOPE), kr, zeros(SLAB - QK_HEAD)], axis=1)
    w_in_p = jnp.concatenate([cq, ckv, xs, krs, gq * (GLA_DK ** -0.5), gk, gv, gg], axis=1)

    pad_head = lambda t: jnp.pad(t, [(0, 0)] * (t.ndim - 1) + [(0, SLAB - t.shape[-1])])
    w_uq_p = pad_head(w_uq[j].reshape(Q_LORA, MLA_HEADS, QK_HEAD)).reshape(Q_LORA, QK_W)
    ukv = w_ukv[j].reshape(KV_LORA, MLA_HEADS, QK_NOPE + V_HEAD)
    w_k = pad_head(ukv[:, :, :QK_NOPE]).reshape(KV_LORA, QK_W)
    w_v = ukv[:, :, QK_NOPE:].reshape(KV_LORA, V_W)
    gain = lambda g: jnp.tile(pad_head(g), MLA_HEADS).reshape(1, QK_W)

    w_al = jnp.zeros((SLAB, 2 * GQ_W), jnp.float32)
    for d in range(2):
        r0 = QK_ROPE + d * ALPHA_RANK
        w_al = w_al.at[r0:r0 + ALPHA_RANK, d * GQ_W:(d + 1) * GQ_W].set(w_alpha_up[j, d])
    return dict(
        w_in=_bf(w_in_p), qa_norm=qa_norm[j].reshape(1, -1), w_uq=_bf(w_uq_p),
        kva_norm=kva_norm[j].reshape(1, -1), w_ukv=_bf(jnp.concatenate([w_k, w_v], axis=1)),
        q_gain=gain(q_norm[j]) * (math.log2(math.e) / math.sqrt(QK_HEAD)), k_gain=gain(k_norm[j]),
        w_alpha=_bf(w_al), b_alpha=b_alpha[j].reshape(1, 2 * GQ_W), gla_norm=gla_norm[j].reshape(1, -1),
        w_out_attn=_bf(w_out[j, :V_W]), w_out_gla=_bf(w_out[j, V_W:]))


def _prep_ffn(i, norm_ffn, ffn_w_gate, ffn_w_up, ffn_w_down):
    return dict(norm_ffn=norm_ffn[i].reshape(1, D), ffn_gate=_bf(ffn_w_gate[i]), ffn_up=_bf(ffn_w_up[i]),
                ffn_down=_bf(ffn_w_down[i]))


def _trunk(x, mods, mod_row, w0, w1, rope_t, cache, s0, gla_consts, nb, tl, tq, nba, nbg, gla_single_buffer):
    q, k, v, gq, gk, gv, gg, la, ckv_n, kr = _mix_in_call(x, mods[0], mod_row, w0, rope_t, nb, tl)
    attn = _attn_call(q, k, v, cache, tq, nba)
    o, s_fin = _gla_call(gq, gk, gv, la, s0, gla_consts, nbg, gla_single_buffer)
    x = _mix_out_call(x, mods[0], mod_row, attn, o, gg, w0, nb, tl)
    u = _conv_in_call(x, mods[1], mod_row, w1, nb, tl)
    x = _conv_out_call(x, mods[1], mod_row, u, w1, nb, tl)
    return x, ckv_n, kr, s_fin


def kernel(x_prompt, x_sample, cache_mla_ckv, cache_mla_krope, state_gla, c, c_ctx, w_ada, b_ada, norm_mix,
           norm_ffn, w_in, qa_norm, w_uq, kva_norm, w_ukv, q_norm, k_norm, w_alpha_up, b_alpha, gla_norm,
           w_out, conv_w_pw1, conv_b_pw1, conv_w_dw, conv_b_dw, conv_ln_g, conv_ln_b, conv_w_pw2,
           conv_b_pw2, ffn_w_gate, ffn_w_up, ffn_w_down):
    dec_batch = c.shape[0]
    ctx_row = dec_batch
    cv = jnp.concatenate([c, c_ctx[None], jnp.zeros((8 - dec_batch - 1, D), c.dtype)], axis=0)
    mods = _ada_call(cv, w_ada, b_ada).reshape(w_ada.shape[0], 8, 6, D)

    w0 = _prep_even(0, w_in, qa_norm, w_uq, kva_norm, w_ukv, q_norm, k_norm, w_alpha_up, b_alpha,
                    gla_norm, w_out)
    w0.update(_prep_ffn(0, norm_ffn, ffn_w_gate, ffn_w_up, ffn_w_down), norm_mix=norm_mix[0].reshape(1, D))
    w1 = dict(norm_mix=norm_mix[1].reshape(1, D), conv_w_pw1=_bf(conv_w_pw1[0]),
              conv_b_pw1=conv_b_pw1[0].reshape(1, -1), conv_w_dw=conv_w_dw[0],
              conv_b_dw=conv_b_dw[0].reshape(1, D), conv_ln_g=conv_ln_g[0].reshape(1, D),
              conv_ln_b=conv_ln_b[0].reshape(1, D), conv_w_pw2=_bf(conv_w_pw2[0]),
              conv_b_pw2=conv_b_pw2[0].reshape(1, D))
    w1.update(_prep_ffn(1, norm_ffn, ffn_w_gate, ffn_w_up, ffn_w_down))
    gla_consts = _gla_consts()

    seq = x_prompt.shape[1]
    y_prompt, ckv_n, kr, s_fin = _trunk(x_prompt, mods, lambda b: ctx_row, w0, w1, None, None, None,
                                        gla_consts, nb=2, tl=seq, tq=seq, nba=4, nbg=2, gla_single_buffer=False)

    rope_t = _rope_tables(x_sample.shape[1])
    krs_c = jnp.pad(cache_mla_krope[:, 0], ((0, 0), (0, 0), (QK_NOPE, SLAB - QK_HEAD)))
    cache = _kv_cache_call(cache_mla_ckv[:, 0], krs_c, w0)
    y_sample, _, _, _ = _trunk(x_sample, mods, lambda b: b, w0, w1, rope_t, cache, state_gla[:, 0],
                               gla_consts, nb=1, tl=512, tq=512, nba=1, nbg=1, gla_single_buffer=True)

    return (y_prompt, y_sample, ckv_n[:, None], kr[:, None], s_fin[:, None])
```

```python
import functools
import math

import numpy as np
import jax
import jax.numpy as jnp
from jax import lax
from jax.experimental import pallas as pl
from jax.experimental.pallas import tpu as pltpu

D = 1024
GRID_W = 64
MLA_HEADS = 8
Q_LORA = 384
KV_LORA = 256
QK_NOPE = 64
QK_ROPE = 32
QK_HEAD = QK_NOPE + QK_ROPE
V_HEAD = 64
ROPE_THETA = 10000.0
GLA_HEADS = 4
GLA_DK = 64
GLA_DV = 128
ALPHA_RANK = 16
GLA_TAU = 16.0
CONV_WIDTH = 31
CONV_PAD = CONV_WIDTH // 2
FFN_HIDDEN = 2816
EPS = 1e-6

LANES = 128
MXU_COLS = 256
VMEM_LIMIT = 56 * 1024 * 1024

SLAB = LANES
QK_W = MLA_HEADS * SLAB
V_W = MLA_HEADS * V_HEAD
GQ_W = GLA_HEADS * GLA_DK
GV_W = GLA_HEADS * GLA_DV
FFN_CHUNK = MXU_COLS
FFN_NCHUNK = FFN_HIDDEN // FFN_CHUNK
HALO = 16
GLA_BLK = 16
GLA_MC = 128
GLA_NBLK = GLA_MC // GLA_BLK
CONV_ROWS = 64
ROW_TILE = 2 * MXU_COLS
SUBLANES = 8
ADA_COLS = 6 * MXU_COLS
ATTN_KCHUNK = 2 * MXU_COLS

_OFF_CQ = 0
_OFF_CKV = _OFF_CQ + Q_LORA
_OFF_XS = _OFF_CKV + KV_LORA
_OFF_KRS = _OFF_XS + SLAB
_OFF_GQ = _OFF_KRS + SLAB
_OFF_GK = _OFF_GQ + GQ_W
_OFF_GV = _OFF_GK + GQ_W
_OFF_GG = _OFF_GV + GV_W
IN_W_PAD = _OFF_GG + GV_W


def _bf(x):
    return x.astype(jnp.bfloat16)


def _dot(a, b):
    return jnp.dot(a, b, preferred_element_type=jnp.float32)


def _dot_nt(a, b):
    return lax.dot_general(a, b, (((1,), (1,)), ((), ())), preferred_element_type=jnp.float32)


def _sigmoid(x):
    return 1.0 / (1.0 + jnp.exp(-x))


def _silu(x):
    return x * _sigmoid(x)


def _rms(x, n=None):
    n = x.shape[-1] if n is None else n
    ss = jnp.sum(x * x, axis=-1, keepdims=True) * (1.0 / n)
    return x * lax.rsqrt(ss + EPS)


def _modulate(x, g, shift, scale):
    return _rms(x) * g * (1.0 + scale) + shift


def _cparams(sem):
    return pltpu.CompilerParams(dimension_semantics=sem, vmem_limit_bytes=VMEM_LIMIT)


def _const_spec(shape):
    nd = len(shape)
    return pl.BlockSpec(shape, lambda *_: (0,) * nd, pipeline_mode=pl.Buffered(1))


def _ada_kernel(c_ref, w_ref, b_ref, o_ref):
    s = _bf(_silu(c_ref[...]))
    o_ref[0] = _dot(s, _bf(w_ref[0])) + b_ref[0]


def _ada_call(cv, w_ada, b_ada):
    depth = w_ada.shape[0]
    tn = ADA_COLS
    return pl.pallas_call(
        _ada_kernel,
        out_shape=jax.ShapeDtypeStruct((depth, SUBLANES, 6 * D), jnp.float32),
        grid=(depth, 6 * D // tn),
        in_specs=[pl.BlockSpec((SUBLANES, D), lambda i, j: (0, 0)),
                  pl.BlockSpec((1, D, tn), lambda i, j: (i, 0, j)),
                  pl.BlockSpec((1, 1, tn), lambda i, j: (i, 0, j))],
        out_specs=pl.BlockSpec((1, SUBLANES, tn), lambda i, j: (i, 0, j)),
        compiler_params=_cparams(("arbitrary", "arbitrary")),
        name="ada_params",
    )(cv, w_ada, b_ada.reshape(depth, 1, 6 * D))


def _rope_kernel(f_ref, isrow_ref, sa_ref, sb_ref, c_ref, s_ref):
    n = c_ref.shape[0]
    t = lax.broadcasted_iota(jnp.int32, (n, SLAB), 0)
    r = (t // GRID_W).astype(jnp.float32)
    col = (t % GRID_W).astype(jnp.float32)
    pos = jnp.where(isrow_ref[...] > 0.5, r, col)
    ang = pos * f_ref[...]
    cs = jnp.cos(ang)
    sn = jnp.sin(ang)
    rope_lane = (sa_ref[...] != 0.0) | (sb_ref[...] != 0.0)
    c_ref[...] = jnp.where(rope_lane, cs, 1.0)
    s_ref[...] = sn * (sa_ref[...] + sb_ref[...])


def _rope_tables(n_tokens):
    half = QK_ROPE // 2
    m = half // 2
    inv_freq = ROPE_THETA ** (-np.arange(0, half, 2, dtype=np.float32) / half)
    f = np.zeros((1, SLAB), np.float32)
    isrow = np.zeros((1, SLAB), np.float32)
    sa = np.zeros((1, SLAB), np.float32)
    sb = np.zeros((1, SLAB), np.float32)
    for axis in range(2):
        base = QK_NOPE + axis * half
        f[0, base:base + m] = inv_freq
        f[0, base + m:base + half] = inv_freq
        isrow[0, base:base + half] = 1.0 if axis == 0 else 0.0
        sa[0, base:base + m] = -1.0
        sb[0, base + m:base + half] = 1.0
    out = jax.ShapeDtypeStruct((n_tokens, SLAB), jnp.float32)
    partner = np.zeros((SLAB, SLAB), np.float32)
    for dst in range(SLAB):
        if sa[0, dst] != 0.0:
            partner[dst + m, dst] = 1.0
        if sb[0, dst] != 0.0:
            partner[dst - m, dst] = 1.0
    c_tab, s_tab = pl.pallas_call(_rope_kernel, out_shape=(out, out), name="rope_tables")(
        jnp.asarray(f), jnp.asarray(isrow), jnp.asarray(sa), jnp.asarray(sb))
    return c_tab, s_tab, jnp.asarray(partner, jnp.bfloat16)


def _head_norm_rope(slab, gain, rope):
    y = _rms(slab, QK_HEAD) * gain
    if rope is not None:
        c, s, partner = rope
        y = y * c + _dot(_bf(y), partner) * s
    return y


def _store(ref, val, lanes=slice(None)):
    if len(ref.shape) == 3:
        ref[:, :, lanes] = val.reshape(ref.shape[0], ref.shape[1], val.shape[-1])
    else:
        ref[:, lanes] = val


def _keys_values(ckv_n, krs, wukv_ref, kg_ref, rope, k_ref, v_ref):
    kv = _dot(_bf(ckv_n), wukv_ref[...])
    for h in range(MLA_HEADS):
        sl = slice(h * SLAB, (h + 1) * SLAB)
        _store(k_ref, _bf(_head_norm_rope(kv[:, sl] + krs, kg_ref[:, sl], rope)), sl)
    _store(v_ref, _bf(kv[:, QK_W:]))


def _mix_in_kernel(*refs, rope, rows):
    (x_ref, mod_ref, nrm_ref, win_ref, qan_ref, wuq_ref, kvan_ref, wukv_ref, qg_ref, kg_ref,
     wal_ref, bal_ref) = refs[:12]
    pos = 12
    rope_t = None
    if rope:
        rope_t = (refs[12][...], refs[13][...], refs[14][...])
        pos = 15
    q_ref, k_ref, v_ref, gq_ref, gk_ref, gv_ref, gg_ref, la_ref, ckv_ref, kr_ref = refs[pos:]

    x = x_ref[...].reshape(rows, D)
    m = mod_ref[0]
    h = _modulate(x, nrm_ref[...], m[0:1], m[1:2])
    proj = _dot(_bf(h), win_ref[...])
    cq = proj[:, _OFF_CQ:_OFF_CQ + Q_LORA]
    ckv = proj[:, _OFF_CKV:_OFF_CKV + KV_LORA]
    xs = proj[:, _OFF_XS:_OFF_XS + SLAB]
    krs = proj[:, _OFF_KRS:_OFF_KRS + SLAB]

    q = _dot(_bf(_rms(cq) * qan_ref[...]), wuq_ref[...])
    for hd in range(MLA_HEADS):
        sl = slice(hd * SLAB, (hd + 1) * SLAB)
        _store(q_ref, _bf(_head_norm_rope(q[:, sl], qg_ref[:, sl], rope_t)), sl)

    ckv_n = _rms(ckv) * kvan_ref[...]
    _store(ckv_ref, ckv_n)
    _store(kr_ref, xs[:, :QK_ROPE])
    _keys_values(ckv_n, krs, wukv_ref, kg_ref, rope_t, k_ref, v_ref)

    _store(gq_ref, _bf(proj[:, _OFF_GQ:_OFF_GQ + GQ_W]))
    _store(gk_ref, _bf(proj[:, _OFF_GK:_OFF_GK + GQ_W]))
    _store(gv_ref, _bf(proj[:, _OFF_GV:_OFF_GV + GV_W]))
    _store(gg_ref, proj[:, _OFF_GG:_OFF_GG + GV_W])
    z = _dot(_bf(xs), wal_ref[...]) + bal_ref[...]
    log_sig = jnp.minimum(z, 0.0) - jnp.log1p(jnp.exp(-jnp.abs(z)))
    _store(la_ref, log_sig * (1.0 / GLA_TAU))


def _mix_in_call(x, mods, mod_row, w, rope_t, nb, tl):
    bsz, seq, _ = x.shape
    assert bsz % nb == 0 and seq % tl == 0
    rows = nb * tl
    grid = (bsz // nb, seq // tl)
    tok = lambda width: pl.BlockSpec((nb, tl, width), lambda b, l: (b, l, 0))
    in_specs = [tok(D),
                pl.BlockSpec((1, 6, D), lambda b, l: (mod_row(b), 0, 0)),
                _const_spec((1, D)), _const_spec((D, IN_W_PAD)), _const_spec((1, Q_LORA)),
                _const_spec((Q_LORA, QK_W)), _const_spec((1, KV_LORA)),
                _const_spec((KV_LORA, QK_W + V_W)), _const_spec((1, QK_W)), _const_spec((1, QK_W)),
                _const_spec((SLAB, 2 * GQ_W)), _const_spec((1, 2 * GQ_W))]
    args = [x, mods, w["norm_mix"], w["w_in"], w["qa_norm"], w["w_uq"], w["kva_norm"], w["w_ukv"],
            w["q_gain"], w["k_gain"], w["w_alpha"], w["b_alpha"]]
    if rope_t is not None:
        assert nb == 1
        in_specs += [pl.BlockSpec((tl, SLAB), lambda b, l: (l, 0))] * 2 + [_const_spec((SLAB, SLAB))]
        args += list(rope_t)
    widths = [(QK_W, jnp.bfloat16), (QK_W, jnp.bfloat16), (V_W, jnp.bfloat16), (GQ_W, jnp.bfloat16),
              (GQ_W, jnp.bfloat16), (GV_W, jnp.bfloat16), (GV_W, jnp.float32), (2 * GQ_W, jnp.float32),
              (KV_LORA, jnp.float32), (QK_ROPE, jnp.float32)]
    return pl.pallas_call(
        functools.partial(_mix_in_kernel, rope=rope_t is not None, rows=rows),
        out_shape=tuple(jax.ShapeDtypeStruct((bsz, seq, wd), dt) for wd, dt in widths),
        grid=grid,
        in_specs=in_specs,
        out_specs=tuple(tok(wd) for wd, _ in widths),
        compiler_params=_cparams(("arbitrary", "arbitrary")),
        name="mix_in",
    )(*args)


def _kv_cache_kernel(ckv_ref, krs_ref, wukv_ref, kg_ref, k_ref, v_ref):
    _keys_values(ckv_ref[0], krs_ref[0], wukv_ref, kg_ref, None, k_ref, v_ref)


def _kv_cache_call(ckv_c, krs_c, w):
    bsz, past, _ = ckv_c.shape
    blk = lambda width: pl.BlockSpec((1, past, width), lambda b: (b, 0, 0))
    return pl.pallas_call(
        _kv_cache_kernel,
        out_shape=(jax.ShapeDtypeStruct((bsz, past, QK_W), jnp.bfloat16),
                   jax.ShapeDtypeStruct((bsz, past, V_W), jnp.bfloat16)),
        grid=(bsz,),
        in_specs=[blk(KV_LORA), blk(SLAB), _const_spec((KV_LORA, QK_W + V_W)), _const_spec((1, QK_W))],
        out_specs=(blk(QK_W), blk(V_W)),
        compiler_params=_cparams(("arbitrary",)),
        name="kv_cache",
    )(ckv_c, krs_c, w["w_ukv"], w["k_gain"])


def _attn_kernel(*refs, has_cache, nba):
    if has_cache:
        q_ref, kn_ref, vn_ref, kc_ref, vc_ref, o_ref, va_ref = refs
    else:
        q_ref, kn_ref, vn_ref, o_ref, va_ref = refs
        kc_ref = vc_ref = None
    lk = kn_ref.shape[1]
    lane = lax.broadcasted_iota(jnp.int32, (1, 2 * V_HEAD), 1)
    lo = lane < V_HEAD

    @pl.when(pl.program_id(2) == 0)
    def _():
        for b in range(nba):
            for h in range(2):
                keep = lo if h == 0 else jnp.logical_not(lo)
                vn = vn_ref[b]
                va_ref[b, h, 0:lk, :] = jnp.where(keep, vn, jnp.ones_like(vn))
                if has_cache:
                    vc = vc_ref[b]
                    va_ref[b, h, lk:, :] = jnp.where(keep, vc, jnp.ones_like(vc))

    chunks = [(kn_ref, c0, c0, min(ATTN_KCHUNK, lk - c0)) for c0 in range(0, lk, ATTN_KCHUNK)]
    if has_cache:
        chunks.append((kc_ref, 0, lk, kc_ref.shape[1]))
    for b in range(nba):
        scores = []
        for h in range(2):
            sl = slice(h * SLAB, (h + 1) * SLAB)
            qh = q_ref[b, :, sl]
            parts = []
            m_el = None
            for k_ref, r0, _, n in chunks:
                s = _dot_nt(qh, k_ref[b, r0:r0 + n, sl])
                parts.append(s)
                for c in range(0, n, LANES):
                    m_el = s[:, c:c + LANES] if m_el is None else jnp.maximum(m_el, s[:, c:c + LANES])
            scores.append((parts, jnp.max(m_el, axis=-1, keepdims=True)))
        outs = []
        for h in range(2):
            parts, mx = scores[h]
            o = None
            for s, (_, _, v0, n) in zip(parts, chunks):
                term = _dot(_bf(jnp.exp2(s - mx)), va_ref[b, h, v0:v0 + n, :])
                o = term if o is None else o + term
            outs.append(o / pltpu.roll(o, V_HEAD, 1))
        o_ref[b] = _bf(jnp.where(lo, outs[0], outs[1]))


def _attn_call(q, kn, vn, cache, tq, nba):
    bsz, seq, _ = q.shape
    assert bsz % nba == 0 and seq % tq == 0
    lk = kn.shape[1]
    npair = MLA_HEADS // 2
    in_specs = [pl.BlockSpec((nba, tq, 2 * SLAB), lambda b, p, i: (b, i, p)),
                pl.BlockSpec((nba, lk, 2 * SLAB), lambda b, p, i: (b, 0, p)),
                pl.BlockSpec((nba, lk, 2 * V_HEAD), lambda b, p, i: (b, 0, p))]
    args = [q, kn, vn]
    past = 0
    if cache is not None:
        past = cache[0].shape[1]
        in_specs += [pl.BlockSpec((nba, past, 2 * SLAB), lambda b, p, i: (b, 0, p)),
                     pl.BlockSpec((nba, past, 2 * V_HEAD), lambda b, p, i: (b, 0, p))]
        args += list(cache)
    return pl.pallas_call(
        functools.partial(_attn_kernel, has_cache=cache is not None, nba=nba),
        out_shape=jax.ShapeDtypeStruct((bsz, seq, V_W), jnp.bfloat16),
        grid=(bsz // nba, npair, seq // tq),
        in_specs=in_specs,
        out_specs=pl.BlockSpec((nba, tq, 2 * V_HEAD), lambda b, p, i: (b, i, p)),
        scratch_shapes=[pltpu.VMEM((nba, 2, lk + past, 2 * V_HEAD), jnp.bfloat16)],
        compiler_params=_cparams(("arbitrary", "arbitrary", "arbitrary")),
        name="mla_attention",
    )(*args)


def _gla_consts():
    t = np.arange(GLA_MC)
    same = (t[:, None] // GLA_BLK) == (t[None, :] // GLA_BLK)
    tri_f = same & (t[None, :] <= t[:, None])
    tri_b = same & (t[None, :] >= t[:, None])
    sums = np.stack([np.concatenate([tri_f, same], 0), np.concatenate([tri_b, same], 0)]).astype(np.float32)
    red = np.zeros((GLA_BLK, 2, GLA_DK, 2, GLA_NBLK, GLA_BLK), np.float32)
    for jj in range(GLA_BLK):
        for hd in range(2):
            red[jj, hd, :, hd, :, jj] = 1.0
    red = red.reshape(GLA_BLK * 2 * GLA_DK, 2 * GLA_MC)
    return jnp.asarray(sums, jnp.bfloat16), jnp.asarray(red, jnp.bfloat16)


def _split3(g):
    g1 = _bf(g)
    r1 = g - g1.astype(jnp.float32)
    g2 = _bf(r1)
    g3 = _bf(r1 - g2.astype(jnp.float32))
    return jnp.concatenate([g1, g2, g3], axis=1)


def _gla_mask(backward):
    row = lax.broadcasted_iota(jnp.int32, (GLA_MC, 2 * GLA_MC), 0)
    colv = lax.broadcasted_iota(jnp.int32, (GLA_MC, 2 * GLA_MC), 1)
    jj_of = colv % GLA_BLK
    in_blk = (colv // GLA_BLK) % GLA_NBLK == row // GLA_BLK
    causal = (jj_of >= row % GLA_BLK) if backward else (jj_of <= row % GLA_BLK)
    return in_blk & causal


def _gla_chunks(chains, sums_ref, red_ref, rows_ref, masks):
    n = len(chains)
    two_dk = 2 * GLA_DK
    dirs = [ch[5] for ch in chains]
    vb = [_bf(ch[3]) for ch in chains]

    parts = [_dot(sums_ref[ch[5]], _split3(ch[2])) for ch in chains]
    parts = [p[:, :two_dk] + p[:, two_dk:2 * two_dk] + p[:, 2 * two_dk:] for p in parts]
    cum = [p[:GLA_MC] for p in parts]
    tot = [p[GLA_MC:] for p in parts]

    qh = [_bf(chains[i][0] * jnp.exp(cum[i])) for i in range(n)]
    kh_t = [_bf(jnp.transpose(chains[i][1] * jnp.exp(tot[i] - cum[i]))) for i in range(n)]
    dec_t = [jnp.transpose(jnp.exp(tot[i])) for i in range(n)]
    blk_lane = lax.broadcasted_iota(jnp.int32, (1, GLA_MC), 1) // GLA_BLK

    upds = [[None] * GLA_NBLK for _ in range(n)]
    for blk in range(GLA_NBLK):
        for i in range(n):
            k_sel = jnp.where(blk_lane == blk, kh_t[i], jnp.zeros_like(kh_t[i]))
            upds[i][blk] = [_dot(k_sel[hd * GLA_DK:(hd + 1) * GLA_DK], vb[i][:, hd * GLA_DV:(hd + 1) * GLA_DV])
                            for hd in range(2)]

    ws = [[] for _ in range(n)]
    for i in range(n):
        rows_ref[i, 0] = chains[i][1]
        rows_ref[i, 1] = cum[i]

    def block_rows(i, which, jj):
        return jnp.concatenate(
            [jnp.broadcast_to(rows_ref[i, which, blk * GLA_BLK + jj:blk * GLA_BLK + jj + 1, :], (GLA_BLK, two_dk))
             for blk in range(GLA_NBLK)], axis=0)

    for jj in range(GLA_BLK):
        for i in range(n):
            e = jnp.exp(jnp.minimum(cum[i] - block_rows(i, 1, jj), 0.0))
            ws[i].append(_bf(chains[i][0] * block_rows(i, 0, jj) * e))
    zeros_v = jnp.zeros((GLA_MC, GLA_DV), jnp.bfloat16)
    o_intra = []
    for i in range(n):
        a = _dot(jnp.concatenate(ws[i], axis=1), red_ref[...])
        a = _bf(jnp.where(masks[dirs[i]], a, 0.0))
        v_bd = jnp.concatenate([jnp.concatenate([vb[i][:, :GLA_DV], zeros_v], axis=1),
                                jnp.concatenate([zeros_v, vb[i][:, GLA_DV:]], axis=1)], axis=0)
        o_intra.append(_dot(a, v_bd))

    o_blocks = [[None] * GLA_NBLK for _ in range(n)]
    zeros_s = jnp.zeros((GLA_DK, GLA_DV), jnp.bfloat16)
    st = [list(ch[4]) for ch in chains]
    for step in range(GLA_NBLK):
        for i in range(n):
            blk = GLA_NBLK - 1 - step if dirs[i] else step
            rs = slice(blk * GLA_BLK, (blk + 1) * GLA_BLK)
            s_bd = jnp.concatenate([jnp.concatenate([_bf(st[i][0]), zeros_s], axis=1),
                                    jnp.concatenate([zeros_s, _bf(st[i][1])], axis=1)], axis=0)
            o_blocks[i][blk] = o_intra[i][rs] + _dot(qh[i][rs], s_bd)
            col = blk * GLA_BLK
            for hd in range(2):
                dcol = dec_t[i][hd * GLA_DK:(hd + 1) * GLA_DK, col:col + 1]
                st[i][hd] = st[i][hd] * dcol + upds[i][blk][hd]
    return [jnp.concatenate(ob, axis=0) for ob in o_blocks], [tuple(s) for s in st]


def _gla_kernel(*refs, has_state, nbg):
    if has_state:
        q_ref, k_ref, v_ref, la_ref, sums_ref, red_ref, s0_ref, o_ref, sf_ref, rows_ref = refs
    else:
        q_ref, k_ref, v_ref, la_ref, sums_ref, red_ref, o_ref, sf_ref, rows_ref = refs
    n_mc = q_ref.shape[1] // GLA_MC
    npair = GLA_HEADS // 2
    two_dk = 2 * GLA_DK
    two_dv = 2 * GLA_DV
    z = jnp.zeros((GLA_DK, GLA_DV), jnp.float32)
    chains = [(bi, hp, d) for bi in range(nbg) for hp in range(npair) for d in range(2)]
    st0 = tuple(tuple(s0_ref[bi, d, 2 * hp + hd] if has_state else z for hd in range(2))
                for bi, hp, d in chains)
    masks = (_gla_mask(False), _gla_mask(True))
    o_ref[...] = jnp.zeros(o_ref.shape, o_ref.dtype)

    def body(c, carry):
        work, where = [], []
        for (bi, hp, d), st in zip(chains, carry):
            r0 = pl.multiple_of(((n_mc - 1 - c) if d else c) * GLA_MC, GLA_MC)
            rows = pl.ds(r0, GLA_MC)
            qk_l = slice(hp * two_dk, (hp + 1) * two_dk)
            v_l = slice(hp * two_dv, (hp + 1) * two_dv)
            la_l = slice(d * GQ_W + hp * two_dk, d * GQ_W + (hp + 1) * two_dk)
            work.append((q_ref[bi, rows, qk_l].astype(jnp.float32), k_ref[bi, rows, qk_l].astype(jnp.float32),
                         la_ref[bi, rows, la_l], v_ref[bi, rows, v_l], st, d))
            where.append((bi, rows, v_l))
        outs, new = _gla_chunks(work, sums_ref, red_ref, rows_ref, masks)
        for (bi, rows, v_l), o in zip(where, outs):
            o_ref[bi, rows, v_l] += o
        return tuple(new)

    finals = lax.fori_loop(0, n_mc, body, st0)
    for (bi, hp, d), st in zip(chains, finals):
        for hd in range(2):
            sf_ref[bi, d, 2 * hp + hd] = st[hd]


def _gla_call(gq, gk, gv, la, s0, consts, nbg, single_buffer):
    bsz, seq, _ = gq.shape
    assert bsz % nbg == 0 and seq % GLA_MC == 0
    sums, red = consts
    mode = dict(pipeline_mode=pl.Buffered(1)) if single_buffer else {}
    tok = lambda width: pl.BlockSpec((nbg, seq, width), lambda b: (b, 0, 0), **mode)
    st_spec = pl.BlockSpec((nbg, 2, GLA_HEADS, GLA_DK, GLA_DV), lambda b: (b, 0, 0, 0, 0))
    in_specs = [tok(GQ_W), tok(GQ_W), tok(GV_W), tok(2 * GQ_W), _const_spec(sums.shape), _const_spec(red.shape)]
    args = [gq, gk, gv, la, sums, red]
    if s0 is not None:
        in_specs.append(st_spec)
        args.append(s0)
    return pl.pallas_call(
        functools.partial(_gla_kernel, has_state=s0 is not None, nbg=nbg),
        out_shape=(jax.ShapeDtypeStruct((bsz, seq, GV_W), jnp.float32),
                   jax.ShapeDtypeStruct((bsz, 2, GLA_HEADS, GLA_DK, GLA_DV), jnp.float32)),
        grid=(bsz // nbg,),
        in_specs=in_specs,
        out_specs=(pl.BlockSpec((nbg, seq, GV_W), lambda b: (b, 0, 0)), st_spec),
        scratch_shapes=[pltpu.VMEM((nbg * GLA_HEADS, 2, GLA_MC, 2 * GLA_DK), jnp.float32)],
        compiler_params=_cparams(("arbitrary",)),
        name="gla_bidir",
    )(*args)


def _ffn(hb, wg_ref, wu_ref, wd_ref):
    acc = None
    for c in range(FFN_NCHUNK):
        hs = slice(c * FFN_CHUNK, (c + 1) * FFN_CHUNK)
        g = _dot(hb, wg_ref[:, hs])
        u = _dot(hb, wu_ref[:, hs])
        part = _dot(_bf(_silu(g) * u), wd_ref[hs, :])
        acc = part if acc is None else acc + part
    return acc


def _ffn_specs():
    return [_const_spec((D, FFN_HIDDEN)), _const_spec((D, FFN_HIDDEN)), _const_spec((FFN_HIDDEN, D))]


def _mix_out_kernel(x_ref, mod_ref, attn_ref, o_ref, gg_ref, gn_ref, woa_ref, wog_ref, nrm_ref,
                    wg_ref, wu_ref, wd_ref, y_ref, *, rows):
    x = x_ref[...].reshape(rows, D)
    m = mod_ref[0]
    o = o_ref[...].reshape(rows, GV_W)
    gg = gg_ref[...].reshape(rows, GV_W)
    out = _dot(attn_ref[...].reshape(rows, V_W), woa_ref[...])
    gated = []
    for h in range(GLA_HEADS):
        sl = slice(h * GLA_DV, (h + 1) * GLA_DV)
        gated.append(_bf(_rms(o[:, sl]) * gn_ref[...] * _silu(gg[:, sl])))
    out = out + _dot(jnp.concatenate(gated, axis=1), wog_ref[...])
    x1 = x + m[2:3] * out
    h2 = _modulate(x1, nrm_ref[...], m[3:4], m[4:5])
    y = x1 + m[5:6] * _ffn(_bf(h2), wg_ref, wu_ref, wd_ref)
    y_ref[...] = y.reshape(y_ref.shape)


def _mix_out_call(x, mods, mod_row, attn, o, gg, w, nb, tl):
    bsz, seq, _ = x.shape
    assert bsz % nb == 0 and seq % tl == 0
    rows = nb * tl
    tok = lambda width: pl.BlockSpec((nb, tl, width), lambda b, l: (b, l, 0))
    return pl.pallas_call(
        functools.partial(_mix_out_kernel, rows=rows),
        out_shape=jax.ShapeDtypeStruct(x.shape, jnp.float32),
        grid=(bsz // nb, seq // tl),
        in_specs=[tok(D), pl.BlockSpec((1, 6, D), lambda b, l: (mod_row(b), 0, 0)),
                  tok(V_W), tok(GV_W), tok(GV_W), _const_spec((1, GLA_DV)),
                  _const_spec((V_W, D)), _const_spec((GV_W, D)), _const_spec((1, D))] + _ffn_specs(),
        out_specs=tok(D),
        compiler_params=_cparams(("arbitrary", "arbitrary")),
        name="mix_out_ffn",
    )(x, mods, attn, o, gg, w["gla_norm"], w["w_out_attn"], w["w_out_gla"], w["norm_ffn"],
      w["ffn_gate"], w["ffn_up"], w["ffn_down"])


def _conv_in_kernel(x_ref, mod_ref, nrm_ref, w1_ref, b1_ref, u_ref, *, rows):
    x = x_ref[...].reshape(rows, D)
    m = mod_ref[0]
    hb = _bf(_modulate(x, nrm_ref[...], m[0:1], m[1:2]))
    a = _dot(hb, w1_ref[:, :D]) + b1_ref[:, :D]
    g = _dot(hb, w1_ref[:, D:]) + b1_ref[:, D:]
    u_ref[...] = (a * _sigmoid(g)).reshape(u_ref.shape)


def _conv_in_call(x, mods, mod_row, w, nb, tl):
    bsz, seq, _ = x.shape
    assert bsz % nb == 0 and seq % tl == 0
    tok = pl.BlockSpec((nb, tl, D), lambda b, l: (b, l, 0))
    return pl.pallas_call(
        functools.partial(_conv_in_kernel, rows=nb * tl),
        out_shape=jax.ShapeDtypeStruct(x.shape, jnp.float32),
        grid=(bsz // nb, seq // tl),
        in_specs=[tok, pl.BlockSpec((1, 6, D), lambda b, l: (mod_row(b), 0, 0)),
                  _const_spec((1, D)), _const_spec((D, 2 * D)), _const_spec((1, 2 * D))],
        out_specs=tok,
        compiler_params=_cparams(("arbitrary", "arbitrary")),
        name="conv_in",
    )(x, mods, w["norm_mix"], w["conv_w_pw1"], w["conv_b_pw1"])


def _conv_out_kernel(*refs, rows, nb, tl, has_halo):
    if has_halo:
        (x_ref, mod_ref, u_ref, up_ref, un_ref, wdw_ref, bdw_ref, lng_ref, lnb_ref, w2_ref, b2_ref,
         nrm_ref, wg_ref, wu_ref, wd_ref, y_ref, ext_ref, act_ref) = refs
    else:
        (x_ref, mod_ref, u_ref, wdw_ref, bdw_ref, lng_ref, lnb_ref, w2_ref, b2_ref,
         nrm_ref, wg_ref, wu_ref, wd_ref, y_ref, ext_ref, act_ref) = refs
    zero_halo = jnp.zeros((nb, HALO, D), jnp.float32)
    ext_ref[:, HALO:HALO + tl, :] = u_ref[...]
    if has_halo:
        li = pl.program_id(1)
        ext_ref[:, 0:HALO, :] = jnp.where(li == 0, zero_halo, up_ref[...])
        ext_ref[:, HALO + tl:, :] = jnp.where(li == pl.num_programs(1) - 1, zero_halo, un_ref[...])
    else:
        ext_ref[:, 0:HALO, :] = zero_halo
        ext_ref[:, HALO + tl:, :] = zero_halo
    win = CONV_ROWS + 2 * HALO
    for b in range(nb):
        for c0 in range(0, tl, CONV_ROWS):
            tiles = []
            for lt in range(D // LANES):
                ls = slice(lt * LANES, (lt + 1) * LANES)
                xw = ext_ref[b, c0:c0 + win, ls]
                acc = None
                for r in range(8):
                    shifted = xw if r == 0 else pltpu.roll(xw, win - r, 0)
                    for a in range(win // 8):
                        tap = 8 * a + r - (HALO - CONV_PAD)
                        if 0 <= tap < CONV_WIDTH:
                            term = shifted[8 * a:8 * a + CONV_ROWS] * wdw_ref[tap:tap + 1, ls]
                            acc = term if acc is None else acc + term
                tiles.append(acc)
            cv = jnp.concatenate(tiles, axis=1) + bdw_ref[...]
            mu = jnp.mean(cv, axis=-1, keepdims=True)
            cc = cv - mu
            var = jnp.mean(cc * cc, axis=-1, keepdims=True)
            ln = cc * lax.rsqrt(var + EPS) * lng_ref[...] + lnb_ref[...]
            act_ref[b * tl + c0:b * tl + c0 + CONV_ROWS, :] = _bf(_silu(ln))
    out = _dot(act_ref[...], w2_ref[...]) + b2_ref[...]
    x = x_ref[...].reshape(rows, D)
    m = mod_ref[0]
    x1 = x + m[2:3] * out
    h2 = _modulate(x1, nrm_ref[...], m[3:4], m[4:5])
    y = x1 + m[5:6] * _ffn(_bf(h2), wg_ref, wu_ref, wd_ref)
    y_ref[...] = y.reshape(y_ref.shape)


def _conv_out_call(x, mods, mod_row, u, w, nb, tl):
    bsz, seq, _ = x.shape
    assert bsz % nb == 0 and seq % tl == 0
    has_halo = tl < seq
    assert has_halo or tl == seq
    assert not (has_halo and nb != 1)
    tok = pl.BlockSpec((nb, tl, D), lambda b, l: (b, l, 0))
    in_specs = [tok, pl.BlockSpec((1, 6, D), lambda b, l: (mod_row(b), 0, 0)), tok]
    args = [x, mods, u]
    if has_halo:
        per = tl // HALO
        nhb = seq // HALO
        in_specs += [pl.BlockSpec((1, HALO, D), lambda b, l: (b, jnp.maximum(l * per - 1, 0), 0)),
                     pl.BlockSpec((1, HALO, D), lambda b, l: (b, jnp.minimum((l + 1) * per, nhb - 1), 0))]
        args += [u, u]
    in_specs += [_const_spec((CONV_WIDTH, D)), _const_spec((1, D)), _const_spec((1, D)), _const_spec((1, D)),
                 _const_spec((D, D)), _const_spec((1, D)), _const_spec((1, D))] + _ffn_specs()
    args += [w["conv_w_dw"], w["conv_b_dw"], w["conv_ln_g"], w["conv_ln_b"], w["conv_w_pw2"],
             w["conv_b_pw2"], w["norm_ffn"], w["ffn_gate"], w["ffn_up"], w["ffn_down"]]
    return pl.pallas_call(
        functools.partial(_conv_out_kernel, rows=nb * tl, nb=nb, tl=tl, has_halo=has_halo),
        out_shape=jax.ShapeDtypeStruct(x.shape, jnp.float32),
        grid=(bsz // nb, seq // tl),
        in_specs=in_specs,
        out_specs=tok,
        scratch_shapes=[pltpu.VMEM((nb, tl + 2 * HALO, D), jnp.float32),
                        pltpu.VMEM((nb * tl, D), jnp.bfloat16)],
        compiler_params=_cparams(("arbitrary", "arbitrary")),
        name="conv_out_ffn",
    )(*args)


def _prep_even(j, w_in, qa_norm, w_uq, kva_norm, w_ukv, q_norm, k_norm, w_alpha_up, b_alpha, gla_norm,
               w_out):
    wi = w_in[j]
    o = np.cumsum((0, Q_LORA, KV_LORA, QK_ROPE, GQ_W, GQ_W, GV_W, GV_W, 2 * ALPHA_RANK))
    cq, ckv, kr, gq, gk, gv, gg, ga = (wi[:, o[i]:o[i + 1]] for i in range(8))
    zeros = lambda n: jnp.zeros((D, n), wi.dtype)
    xs = jnp.concatenate([kr, ga, zeros(SLAB - QK_ROPE - 2 * ALPHA_RANK)], axis=1)
    krs = jnp.concatenate([zeros(QK_NOPE), kr, zeros(SLAB - QK_HEAD)], axis=1)
    w_in_p = jnp.concatenate([cq, ckv, xs, krs, gq * (GLA_DK ** -0.5), gk, gv, gg], axis=1)

    pad_head = lambda t: jnp.pad(t, [(0, 0)] * (t.ndim - 1) + [(0, SLAB - t.shape[-1])])
    w_uq_p = pad_head(w_uq[j].reshape(Q_LORA, MLA_HEADS, QK_HEAD)).reshape(Q_LORA, QK_W)
    ukv = w_ukv[j].reshape(KV_LORA, MLA_HEADS, QK_NOPE + V_HEAD)
    w_k = pad_head(ukv[:, :, :QK_NOPE]).reshape(KV_LORA, QK_W)
    w_v = ukv[:, :, QK_NOPE:].reshape(KV_LORA, V_W)
    gain = lambda g: jnp.tile(pad_head(g), MLA_HEADS).reshape(1, QK_W)

    w_al = jnp.zeros((SLAB, 2 * GQ_W), jnp.float32)
    for d in range(2):
        r0 = QK_ROPE + d * ALPHA_RANK
        w_al = w_al.at[r0:r0 + ALPHA_RANK, d * GQ_W:(d + 1) * GQ_W].set(w_alpha_up[j, d])
    return dict(
        w_in=_bf(w_in_p), qa_norm=qa_norm[j].reshape(1, -1), w_uq=_bf(w_uq_p),
        kva_norm=kva_norm[j].reshape(1, -1), w_ukv=_bf(jnp.concatenate([w_k, w_v], axis=1)),
        q_gain=gain(q_norm[j]) * (math.log2(math.e) / math.sqrt(QK_HEAD)), k_gain=gain(k_norm[j]),
        w_alpha=_bf(w_al), b_alpha=b_alpha[j].reshape(1, 2 * GQ_W), gla_norm=gla_norm[j].reshape(1, -1),
        w_out_attn=_bf(w_out[j, :V_W]), w_out_gla=_bf(w_out[j, V_W:]))


def _prep_ffn(i, norm_ffn, ffn_w_gate, ffn_w_up, ffn_w_down):
    return dict(norm_ffn=norm_ffn[i].reshape(1, D), ffn_gate=_bf(ffn_w_gate[i]), ffn_up=_bf(ffn_w_up[i]),
                ffn_down=_bf(ffn_w_down[i]))


def _tile_plan(bsz, seq):
    tl = min(seq, ROW_TILE)
    nb = ROW_TILE // tl
    nba = min(bsz, max(1, 2 * ROW_TILE // seq))
    gla_bytes = nb * seq * ((2 * GQ_W + GV_W) * 2 + 2 * GQ_W * 4 + GV_W * 4)
    return dict(nb=nb, tl=tl, tq=tl, nba=nba, nbg=nb, gla_single_buffer=2 * gla_bytes > VMEM_LIMIT // 2)


def _trunk(x, mods, mod_row, w0, w1, rope_t, cache, s0, gla_consts, nb, tl, tq, nba, nbg, gla_single_buffer):
    q, k, v, gq, gk, gv, gg, la, ckv_n, kr = _mix_in_call(x, mods[0], mod_row, w0, rope_t, nb, tl)
    attn = _attn_call(q, k, v, cache, tq, nba)
    o, s_fin = _gla_call(gq, gk, gv, la, s0, gla_consts, nbg, gla_single_buffer)
    x = _mix_out_call(x, mods[0], mod_row, attn, o, gg, w0, nb, tl)
    u = _conv_in_call(x, mods[1], mod_row, w1, nb, tl)
    x = _conv_out_call(x, mods[1], mod_row, u, w1, nb, tl)
    return x, ckv_n, kr, s_fin


def kernel(x_prompt, x_sample, cache_mla_ckv, cache_mla_krope, state_gla, c, c_ctx, w_ada, b_ada, norm_mix,
           norm_ffn, w_in, qa_norm, w_uq, kva_norm, w_ukv, q_norm, k_norm, w_alpha_up, b_alpha, gla_norm,
           w_out, conv_w_pw1, conv_b_pw1, conv_w_dw, conv_b_dw, conv_ln_g, conv_ln_b, conv_w_pw2,
           conv_b_pw2, ffn_w_gate, ffn_w_up, ffn_w_down):
    dec_batch = c.shape[0]
    ctx_row = dec_batch
    assert dec_batch < SUBLANES
    cv = jnp.concatenate([c, c_ctx[None], jnp.zeros((SUBLANES - dec_batch - 1, D), c.dtype)], axis=0)
    mods = _ada_call(cv, w_ada, b_ada).reshape(w_ada.shape[0], SUBLANES, 6, D)

    w0 = _prep_even(0, w_in, qa_norm, w_uq, kva_norm, w_ukv, q_norm, k_norm, w_alpha_up, b_alpha,
                    gla_norm, w_out)
    w0.update(_prep_ffn(0, norm_ffn, ffn_w_gate, ffn_w_up, ffn_w_down), norm_mix=norm_mix[0].reshape(1, D))
    w1 = dict(norm_mix=norm_mix[1].reshape(1, D), conv_w_pw1=_bf(conv_w_pw1[0]),
              conv_b_pw1=conv_b_pw1[0].reshape(1, -1), conv_w_dw=conv_w_dw[0],
              conv_b_dw=conv_b_dw[0].reshape(1, D), conv_ln_g=conv_ln_g[0].reshape(1, D),
              conv_ln_b=conv_ln_b[0].reshape(1, D), conv_w_pw2=_bf(conv_w_pw2[0]),
              conv_b_pw2=conv_b_pw2[0].reshape(1, D))
    w1.update(_prep_ffn(1, norm_ffn, ffn_w_gate, ffn_w_up, ffn_w_down))
    gla_consts = _gla_consts()

    y_prompt, ckv_n, kr, s_fin = _trunk(x_prompt, mods, lambda b: ctx_row, w0, w1, None, None, None,
                                        gla_consts, **_tile_plan(*x_prompt.shape[:2]))

    rope_t = _rope_tables(x_sample.shape[1])
    krs_c = jnp.pad(cache_mla_krope[:, 0], ((0, 0), (0, 0), (QK_NOPE, SLAB - QK_HEAD)))
    cache = _kv_cache_call(cache_mla_ckv[:, 0], krs_c, w0)
    y_sample, _, _, _ = _trunk(x_sample, mods, lambda b: b, w0, w1, rope_t, cache, state_gla[:, 0],
                               gla_consts, **_tile_plan(*x_sample.shape[:2]))

    return (y_prompt, y_sample, ckv_n[:, None], kr[:, None], s_fin[:, None])
```
